```python
import math
import jax, jax.numpy as jnp
from jax import lax
import numpy as np

D_MODEL = 1024
BATCH = 8
SEQ = 4096
DEPTH = 1

ATTN_HEAD_DIM = 64
ATTN_WIDTH = D_MODEL // 2
ATTN_HEADS = ATTN_WIDTH // ATTN_HEAD_DIM
DILATED_PATTERNS = ((128, 1), (512, 4), (2048, 16))
ATTN_BLOCK = 128
HGRN_HEAD_DIM = 128
HGRN_WIDTH = D_MODEL - ATTN_WIDTH
HGRN_HEADS = HGRN_WIDTH // HGRN_HEAD_DIM
HGRN_CHUNK = 64
MIX_WIDTH = ATTN_WIDTH + HGRN_WIDTH
IN_PROJ_WIDTH = 3 * ATTN_WIDTH + 4 * HGRN_WIDTH
D_FF = 4 * D_MODEL
RMS_EPS = 1e-6

kernel_name = "hymba_dilated_attn_hgrn2_sqrelu"


def rmsnorm(x, gain):
    xf = x.astype(jnp.float32)
    y = xf * lax.rsqrt(jnp.mean(xf * xf, axis=-1, keepdims=True) + RMS_EPS)
    return (y * gain.astype(jnp.float32)).astype(x.dtype)


def alibi_slopes(n_heads):
    return jnp.exp2(-8.0 * jnp.arange(1, n_heads + 1, dtype=jnp.float32) / n_heads)


def dilated_branch(q, k, v, slopes, window, dilation):
    B, S, H, Dh = q.shape
    span = window // dilation
    L = S // dilation
    nb = -(-L // ATTN_BLOCK)
    Lp = nb * ATTN_BLOCK
    Bd = B * dilation

    def to_sub(t):
        t = t.reshape(B, L, dilation, H, Dh).transpose(0, 2, 1, 3, 4).reshape(Bd, L, H, Dh)
        t = jnp.pad(t, ((0, 0), (0, Lp - L), (0, 0), (0, 0)))
        return t.reshape(Bd, nb, ATTN_BLOCK, H, Dh)

    def with_prev(t):
        prev = jnp.pad(t[:, :-1], ((0, 0), (1, 0), (0, 0), (0, 0), (0, 0)))
        return jnp.concatenate([prev, t], axis=2)

    qb = to_sub(q)
    kk = with_prev(to_sub(k))
    vv = with_prev(to_sub(v))
    s = jnp.einsum('bnqhd,bnkhd->bnhqk', qb, kk).astype(jnp.float32) * (Dh ** -0.5)
    qi = jnp.arange(ATTN_BLOCK)[:, None] + ATTN_BLOCK
    kj = jnp.arange(2 * ATTN_BLOCK)[None, :]
    dist = qi - kj
    blk = jnp.arange(nb)[:, None, None]
    valid = (dist >= 0) & (dist <= span) & (blk * ATTN_BLOCK + kj - ATTN_BLOCK >= 0)
    bias = -slopes[:, None, None] * (dist * dilation).astype(jnp.float32)
    s = jnp.where(valid[None, :, None], s + bias[None, None], -jnp.inf)
    m = jnp.max(s, axis=-1, keepdims=True)
    p = jnp.exp(s - m)
    den = jnp.sum(p, axis=-1)
    o = jnp.einsum('bnhqk,bnkhd->bnqhd', p, vv.astype(jnp.float32))
    o = o / jnp.transpose(den, (0, 1, 3, 2))[..., None]
    lse = jnp.transpose(m[..., 0] + jnp.log(den), (0, 1, 3, 2))

    def from_sub(t):
        t = t.reshape((Bd, Lp) + t.shape[3:])[:, :L]
        t = t.reshape((B, dilation, L) + t.shape[2:])
        t = jnp.swapaxes(t, 1, 2)
        return t.reshape((B, S) + t.shape[3:])

    return from_sub(o), from_sub(lse)


def dilated_mixture_attention(q, k, v, slopes):
    outs, lses = [], []
    for window, dilation in DILATED_PATTERNS:
        o, lse = dilated_branch(q, k, v, slopes, window, dilation)
        outs.append(o)
        lses.append(lse)
    w = jax.nn.softmax(jnp.stack(lses, axis=0), axis=0)
    o = jnp.sum(w[..., None] * jnp.stack(outs, axis=0), axis=0)
    return o.astype(q.dtype)


def hgrn2_recurrence(q, f_pre, i, gate, lower_bound, out_gain):
    B, S, _ = q.shape
    H, D, C = HGRN_HEADS, HGRN_HEAD_DIM, HGRN_CHUNK
    shape = (B, S, H, D)
    lb = lower_bound.reshape(H, D)
    qf = jax.nn.silu(q.astype(jnp.float32)).reshape(shape)
    log_f = jnp.logaddexp(jnp.log(lb), jnp.log1p(-lb) + jax.nn.log_sigmoid(f_pre.astype(jnp.float32).reshape(shape)))
    kf = -jnp.expm1(log_f)
    vf = i.astype(jnp.float32).reshape(shape)
    nC = S // C

    def to_chunks(t):
        return t.reshape(B, nC, C, H, D).transpose(1, 0, 3, 2, 4)

    causal = jnp.tril(jnp.ones((C, C), dtype=bool))[:, :, None]

    def step(state, xs):
        qc, kc, vc, gc = xs
        b = jnp.cumsum(gc, axis=2)
        o_inter = jnp.einsum('bhtk,bhkv->bhtv', qc * jnp.exp(b), state)
        diff = b[:, :, :, None, :] - b[:, :, None, :, :]
        decay = jnp.exp(jnp.where(causal, diff, -jnp.inf))
        a = jnp.einsum('bhtk,bhsk,bhtsk->bhts', qc, kc, decay)
        o = o_inter + jnp.einsum('bhts,bhsv->bhtv', a, vc)
        b_last = b[:, :, -1:, :]
        new_state = jnp.exp(b_last[:, :, 0, :, None]) * state + jnp.einsum(
            'bhsk,bhsv->bhkv', kc * jnp.exp(b_last - b), vc)
        return new_state, o

    state0 = jnp.zeros((B, H, D, D), jnp.float32)
    _, o = lax.scan(step, state0, (to_chunks(qf), to_chunks(kf), to_chunks(vf), to_chunks(log_f)))
    o = o.transpose(1, 0, 3, 2, 4).reshape(shape)
    o = rmsnorm(o, out_gain.reshape(H, D)).reshape(B, S, H * D)
    return (o * jax.nn.silu(gate.astype(jnp.float32))).astype(q.dtype)


def setup_inputs(seed: int = 0) -> dict:
    key = jax.random.key(seed)
    ks = jax.random.split(key, 12)
    f32 = jnp.float32

    def gain(k, shape):
        return 1.0 + 0.02 * jax.random.normal(k, shape, f32)

    return {
        "x": jax.random.normal(ks[0], (BATCH, SEQ, D_MODEL), f32),
        "mix_pre_norm": gain(ks[1], (DEPTH, D_MODEL)),
        "w_in": jax.random.normal(ks[2], (DEPTH, D_MODEL, IN_PROJ_WIDTH), f32) * D_MODEL ** -0.5,
        "attn_out_norm": gain(ks[3], (DEPTH, ATTN_WIDTH)),
        "hgrn_lb_logits": 0.5 * jax.random.normal(ks[4], (DEPTH + 1, HGRN_WIDTH), f32),
        "hgrn_out_norm": gain(ks[5], (DEPTH, HGRN_WIDTH)),
        "w_out": jax.random.normal(ks[6], (DEPTH, MIX_WIDTH, D_MODEL), f32) * MIX_WIDTH ** -0.5,
        "mix_post_norm": gain(ks[7], (DEPTH, D_MODEL)),
        "mlp_pre_norm": gain(ks[8], (DEPTH, D_MODEL)),
        "w_ff1": jax.random.normal(ks[9], (DEPTH, D_MODEL, D_FF), f32) * D_MODEL ** -0.5,
        "w_ff2": jax.random.normal(ks[10], (DEPTH, D_FF, D_MODEL), f32) * D_FF ** -0.5,
        "mlp_post_norm": gain(ks[11], (DEPTH, D_MODEL)),
    }


def reference(x, mix_pre_norm, w_in, attn_out_norm, hgrn_lb_logits, hgrn_out_norm, w_out,
              mix_post_norm, mlp_pre_norm, w_ff1, w_ff2, mlp_post_norm):
    B, S, _ = x.shape
    slopes = alibi_slopes(ATTN_HEADS)
    lower_bounds = jnp.cumsum(jax.nn.softmax(hgrn_lb_logits.astype(jnp.float32), axis=0), axis=0)
    aw, hw = ATTN_WIDTH, HGRN_WIDTH
    splits = [aw, 2 * aw, 3 * aw, 3 * aw + hw, 3 * aw + 2 * hw, 3 * aw + 3 * hw]
    for layer in range(DEPTH):
        h = rmsnorm(x, mix_pre_norm[layer])
        proj = h @ w_in[layer]
        q_a, k_a, v_a, q_h, f_h, i_h, g_h = jnp.split(proj, splits, axis=-1)
        ahead = (B, S, ATTN_HEADS, ATTN_HEAD_DIM)
        attn = dilated_mixture_attention(q_a.reshape(ahead), k_a.reshape(ahead), v_a.reshape(ahead), slopes)
        attn = rmsnorm(attn.reshape(B, S, ATTN_WIDTH), attn_out_norm[layer])
        rec = hgrn2_recurrence(q_h, f_h, i_h, g_h, lower_bounds[layer], hgrn_out_norm[layer])
        mixed = jnp.concatenate([attn, rec], axis=-1) @ w_out[layer]
        x = x + rmsnorm(mixed, mix_post_norm[layer])
        h = rmsnorm(x, mlp_pre_norm[layer])
        ff = jnp.square(jax.nn.relu(h @ w_ff1[layer])) @ w_ff2[layer]
        x = x + rmsnorm(ff, mlp_post_norm[layer])
    return x
```

```python
import functools

import numpy as np
import jax
import jax.numpy as jnp
from jax import lax
from jax.experimental import pallas as pl
from jax.experimental.pallas import tpu as pltpu

F32 = jnp.float32
BF16 = jnp.bfloat16

RMS_EPS = 1e-6
ATTN_HEAD_DIM = 64
ATTN_BLOCK = 128
DILATED_PATTERNS = ((128, 1), (512, 4), (2048, 16))
HGRN_HEAD_DIM = 128
HGRN_CHUNK = 64
LANES = 128
MASKED_SCORE = -1e30
VMEM_LIMIT_BYTES = 56 * 1024 * 1024
ROW_TILE = 512


def _rms(x, gain):
    return x * lax.rsqrt(jnp.mean(x * x, axis=-1, keepdims=True) + RMS_EPS) * gain


def _dot(a, b):
    return jnp.dot(a, b, preferred_element_type=F32)


def _dot_nt(a, b):
    return lax.dot_general(a, b, (((1,), (1,)), ((), ())), preferred_element_type=F32)


def _dot_tn(a, b):
    return lax.dot_general(a, b, (((0,), (0,)), ((), ())), preferred_element_type=F32)


def _inproj_kernel(x_ref, g_ref, w_ref, qkv_ref, hg_ref, *, attn_cols):
    h = _rms(x_ref[...], g_ref[...]).astype(BF16)
    qkv_ref[...] = _dot(h, w_ref[:, :attn_cols])
    hg_ref[...] = _dot(h, w_ref[:, attn_cols:])


def _inproj(x2, gain, w_in_b, attn_cols):
    n, d = x2.shape
    width = w_in_b.shape[1]
    return pl.pallas_call(
        functools.partial(_inproj_kernel, attn_cols=attn_cols),
        grid=(n // ROW_TILE,),
        in_specs=[
            pl.BlockSpec((ROW_TILE, d), lambda i: (i, 0)),
            pl.BlockSpec((1, d), lambda i: (0, 0)),
            pl.BlockSpec((d, width), lambda i: (0, 0), pipeline_mode=pl.Buffered(1)),
        ],
        out_specs=[
            pl.BlockSpec((ROW_TILE, attn_cols), lambda i: (i, 0)),
            pl.BlockSpec((ROW_TILE, width - attn_cols), lambda i: (i, 0)),
        ],
        out_shape=[
            jax.ShapeDtypeStruct((n, attn_cols), F32),
            jax.ShapeDtypeStruct((n, width - attn_cols), F32),
        ],
        compiler_params=pltpu.CompilerParams(
            dimension_semantics=("arbitrary",), vmem_limit_bytes=VMEM_LIMIT_BYTES),
    )(x2, gain, w_in_b)


def _attn_kernel(q_ref, k_ref, v_ref, o_ref, m_sc, ad_sc, bias_sc, *, seq):
    hp = pl.program_id(1)
    blk = ATTN_BLOCK
    ii = lax.broadcasted_iota(jnp.int32, (blk, blk), 0)
    jj = lax.broadcasted_iota(jnp.int32, (blk, blk), 1)
    lane = lax.broadcasted_iota(jnp.int32, (blk, LANES), 1)
    dist_cur = (ii - jj).astype(F32)
    dist_prev = dist_cur + float(blk)
    head_lanes = [lane < ATTN_HEAD_DIM, lane >= ATTN_HEAD_DIM]

    for bi, (_, dil) in enumerate(DILATED_PATTERNS):
        for j in range(2):
            expo = jnp.full((blk, blk), -(2 * hp + j + 1), jnp.int32).astype(F32)
            slope = jnp.exp2(expo) * float(dil)
            bias_sc[bi, j, 0] = jnp.where(jj >= ii, -slope * dist_prev, MASKED_SCORE)
            bias_sc[bi, j, 1] = jnp.where(jj <= ii, -slope * dist_cur, MASKED_SCORE)

    def block(bi, dil, start, has_prev):
        rows_c = pl.ds(start, blk, stride=dil) if dil > 1 else pl.ds(start, blk)
        qb = q_ref[0, rows_c, :] * (ATTN_HEAD_DIM ** -0.5)
        kc = k_ref[0, rows_c, :].astype(BF16)
        vc = v_ref[0, rows_c, :]
        if has_prev:
            pstart = start - blk * dil
            rows_p = pl.ds(pstart, blk, stride=dil) if dil > 1 else pl.ds(pstart, blk)
            kp = k_ref[0, rows_p, :].astype(BF16)
            vp = v_ref[0, rows_p, :]
        results = []
        for j in range(2):
            qh = jnp.where(head_lanes[j], qb, 0.0).astype(BF16)
            s_c = _dot_nt(qh, kc) + bias_sc[bi, j, 1]
            mx = jnp.max(s_c, axis=-1, keepdims=True)
            if has_prev:
                s_p = _dot_nt(qh, kp) + bias_sc[bi, j, 0]
                mx = jnp.maximum(mx, jnp.max(s_p, axis=-1, keepdims=True))
            vsrc = pltpu.roll(vc, ATTN_HEAD_DIM, 1) if j else vc
            vx = jnp.where(head_lanes[0], vsrc, 1.0).astype(BF16)
            ad = _dot(jnp.exp(s_c - mx).astype(BF16), vx)
            if has_prev:
                vsrc = pltpu.roll(vp, ATTN_HEAD_DIM, 1) if j else vp
                vx = jnp.where(head_lanes[0], vsrc, 1.0).astype(BF16)
                ad = ad + _dot(jnp.exp(s_p - mx).astype(BF16), vx)
            mb = jnp.broadcast_to(mx, (blk, LANES))
            if bi > 0:
                m_old = m_sc[j, rows_c, :]
                ad_old = ad_sc[j, rows_c, :]
                m_new = jnp.maximum(m_old, mb)
                ad = ad_old * jnp.exp(m_old - m_new) + ad * jnp.exp(mb - m_new)
                mb = m_new
            if bi < len(DILATED_PATTERNS) - 1:
                m_sc[j, rows_c, :] = mb
                ad_sc[j, rows_c, :] = ad
            else:
                results.append(ad / pltpu.roll(ad, ATTN_HEAD_DIM, 1))
        if results:
            o_ref[0, rows_c, :] = jnp.where(
                head_lanes[0], results[0], pltpu.roll(results[1], ATTN_HEAD_DIM, 1))

    for bi, (window, dil) in enumerate(DILATED_PATTERNS):
        nblocks = seq // dil // blk

        def residue(r, carry, bi=bi, dil=dil, nblocks=nblocks):
            block(bi, dil, r, False)

            def body(n, c):
                block(bi, dil, r + n * (blk * dil), True)
                return c

            lax.fori_loop(1, nblocks, body, 0)
            return carry

        lax.fori_loop(0, dil, residue, 0)


def _attention(qkv, batch, seq):
    width = qkv.shape[-1] // 3
    pairs = width // LANES
    blk = ATTN_BLOCK
    return pl.pallas_call(
        functools.partial(_attn_kernel, seq=seq),
        grid=(batch, pairs),
        in_specs=[
            pl.BlockSpec((1, seq, LANES), lambda b, h: (b, 0, h)),
            pl.BlockSpec((1, seq, LANES), lambda b, h: (b, 0, pairs + h)),
            pl.BlockSpec((1, seq, LANES), lambda b, h: (b, 0, 2 * pairs + h)),
        ],
        out_specs=pl.BlockSpec((1, seq, LANES), lambda b, h: (b, 0, h)),
        out_shape=jax.ShapeDtypeStruct((batch, seq, width), F32),
        scratch_shapes=[
            pltpu.VMEM((2, seq, LANES), F32),
            pltpu.VMEM((2, seq, LANES), F32),
            pltpu.VMEM((len(DILATED_PATTERNS), 2, 2, blk, blk), F32),
        ],
        compiler_params=pltpu.CompilerParams(
            dimension_semantics=("arbitrary", "arbitrary"), vmem_limit_bytes=VMEM_LIMIT_BYTES),
    )(qkv, qkv, qkv)


def _hgrn_tables(chunk):
    levels = int(np.log2(chunk))
    t = np.arange(chunk)
    sums = np.zeros((levels + 2, chunk, chunk), np.float32)
    masks = np.zeros((levels + 1, chunk, chunk), np.float32)
    for l in range(levels):
        m = 1 << l
        blk = t // m
        bdry = (blk | 1) * m - 1
        for tok in range(chunk):
            if blk[tok] & 1:
                sums[l, tok, bdry[tok] + 1: tok + 1] = 1.0
            else:
                sums[l, tok, tok + 1: bdry[tok] + 1] = 1.0
        masks[l] = ((blk[:, None] & 1) == 1) & (blk[None, :] == blk[:, None] - 1)
    sums[levels] = np.tril(np.ones((chunk, chunk), np.float32))
    sums[levels + 1] = np.triu(np.ones((chunk, chunk), np.float32), 1)
    masks[levels] = np.eye(chunk, dtype=np.float32)
    return sums.reshape((levels + 2) * chunk, chunk), masks


def _hgrn_kernel(q_ref, f_ref, i_ref, gate_ref, lbl_ref, gain_ref, sums_ref, masks_ref,
                 o_ref, *, seq, chunk, layer):
    levels = masks_ref.shape[0] - 1
    logits = lbl_ref[...]
    mx = jnp.max(logits, axis=0, keepdims=True)
    e = jnp.exp(logits - mx)
    lb = jnp.sum(e[0:layer + 1, :], axis=0, keepdims=True) / jnp.sum(e, axis=0, keepdims=True)
    gain = gain_ref[...]

    def step(c, state_t):
        rows = pl.ds(pl.multiple_of(c * chunk, chunk), chunk)
        qr = q_ref[0, rows, :]
        fr = f_ref[0, rows, :]
        v = i_ref[0, rows, :].astype(BF16)
        qf = qr * jax.nn.sigmoid(qr)
        f = lb + (1.0 - lb) * jax.nn.sigmoid(fr)
        kf = (1.0 - lb) * jax.nn.sigmoid(-fr)
        g = jnp.log(f)
        g_hi = g.astype(BF16)
        r1 = g - g_hi.astype(F32)
        g_mid = r1.astype(BF16)
        g_lo = (r1 - g_mid.astype(F32)).astype(BF16)
        sums = sums_ref[...]
        ex = _dot(sums, g_hi) + _dot(sums, g_mid) + _dot(sums, g_lo)

        a = masks_ref[levels] * _dot_nt(qf.astype(BF16), kf.astype(BF16))
        for l in range(levels):
            dec = jnp.exp(ex[l * chunk:(l + 1) * chunk])
            a = a + masks_ref[l] * _dot_nt((qf * dec).astype(BF16), (kf * dec).astype(BF16))
        b = ex[levels * chunk:(levels + 1) * chunk]
        to_end = ex[(levels + 1) * chunk:(levels + 2) * chunk]
        o = _dot(a.astype(BF16), v) + _dot_nt((qf * jnp.exp(b)).astype(BF16), state_t.astype(BF16))
        k_end = (kf * jnp.exp(to_end)).astype(BF16)
        state_t = state_t * jnp.exp(b[chunk - 1:chunk, :]) + _dot_tn(v, k_end)

        gt = gate_ref[0, rows, :]
        o_ref[0, rows, :] = _rms(o, gain) * (gt * jax.nn.sigmoid(gt))
        return state_t

    lax.fori_loop(0, seq // chunk, step, jnp.zeros((HGRN_HEAD_DIM, HGRN_HEAD_DIM), F32))


def _hgrn(hg, lb_logits, out_gain, batch, seq, layer):
    width = hg.shape[-1] // 4
    heads = width // HGRN_HEAD_DIM
    chunk = HGRN_CHUNK
    sums, masks = _hgrn_tables(chunk)
    nrows = lb_logits.shape[0]

    def col(k):
        return pl.BlockSpec((1, seq, HGRN_HEAD_DIM), lambda b, h, k=k: (b, 0, k * heads + h))

    return pl.pallas_call(
        functools.partial(_hgrn_kernel, seq=seq, chunk=chunk, layer=layer),
        grid=(batch, heads),
        in_specs=[
            col(0), col(1), col(2), col(3),
            pl.BlockSpec((nrows, HGRN_HEAD_DIM), lambda b, h: (0, h)),
            pl.BlockSpec((1, HGRN_HEAD_DIM), lambda b, h: (0, h)),
            pl.BlockSpec(sums.shape, lambda b, h: (0, 0)),
            pl.BlockSpec(masks.shape, lambda b, h: (0, 0, 0)),
        ],
        out_specs=pl.BlockSpec((1, seq, HGRN_HEAD_DIM), lambda b, h: (b, 0, h)),
        out_shape=jax.ShapeDtypeStruct((batch, seq, width), F32),
        compiler_params=pltpu.CompilerParams(
            dimension_semantics=("arbitrary", "arbitrary"), vmem_limit_bytes=VMEM_LIMIT_BYTES),
    )(hg, hg, hg, hg, lb_logits, out_gain, jnp.asarray(sums, BF16), jnp.asarray(masks, F32))


def _out_ffn_kernel(x_ref, a_ref, r_ref, ga_ref, wo_ref, gpost_ref, gpre_ref, w1_ref, w2_ref,
                    gpost2_ref, o_ref, *, ff_tile):
    aw = a_ref.shape[1]
    an = _rms(a_ref[...], ga_ref[...]).astype(BF16)
    mixed = _dot(an, wo_ref[:aw, :]) + _dot(r_ref[...].astype(BF16), wo_ref[aw:, :])
    x1 = x_ref[...] + _rms(mixed, gpost_ref[...])
    h = _rms(x1, gpre_ref[...]).astype(BF16)
    d_ff = w1_ref.shape[1]
    acc = jnp.zeros(x1.shape, F32)
    for j in range(d_ff // ff_tile):
        u = _dot(h, w1_ref[:, j * ff_tile:(j + 1) * ff_tile])
        u = jnp.square(jnp.maximum(u, 0.0)).astype(BF16)
        acc = acc + _dot(u, w2_ref[j * ff_tile:(j + 1) * ff_tile, :])
    o_ref[...] = x1 + _rms(acc, gpost2_ref[...])


def _out_ffn(x2, attn, rec, ga, wo_b, gpost, gpre, w1_b, w2_b, gpost2):
    n, d = x2.shape
    aw = attn.shape[1]
    rw = rec.shape[1]
    d_ff = w1_b.shape[1]

    def rows(w):
        return pl.BlockSpec((ROW_TILE, w), lambda i: (i, 0))

    def whole(shape):
        return pl.BlockSpec(shape, lambda i: (0, 0), pipeline_mode=pl.Buffered(1))

    return pl.pallas_call(
        functools.partial(_out_ffn_kernel, ff_tile=1024),
        grid=(n // ROW_TILE,),
        in_specs=[
            rows(d), rows(aw), rows(rw), whole((1, aw)), whole((aw + rw, d)), whole((1, d)),
            whole((1, d)), whole((d, d_ff)), whole((d_ff, d)), whole((1, d)),
        ],
        out_specs=rows(d),
        out_shape=jax.ShapeDtypeStruct((n, d), F32),
        compiler_params=pltpu.CompilerParams(
            dimension_semantics=("arbitrary",), vmem_limit_bytes=VMEM_LIMIT_BYTES),
    )(x2, attn, rec, ga, wo_b, gpost, gpre, w1_b, w2_b, gpost2)


def kernel(x, mix_pre_norm, w_in, attn_out_norm, hgrn_lb_logits, hgrn_out_norm, w_out,
           mix_post_norm, mlp_pre_norm, w_ff1, w_ff2, mlp_post_norm):
    batch, seq, d = x.shape
    depth = w_in.shape[0]
    aw = attn_out_norm.shape[1]
    n = batch * seq
    x2 = x.reshape(n, d)
    for layer in range(depth):
        qkv, hg = _inproj(x2, mix_pre_norm[layer][None], w_in[layer].astype(BF16), 3 * aw)
        attn = _attention(qkv.reshape(batch, seq, -1), batch, seq)
        rec = _hgrn(hg.reshape(batch, seq, -1), hgrn_lb_logits, hgrn_out_norm[layer][None],
                    batch, seq, layer)
        x2 = _out_ffn(x2, attn.reshape(n, -1), rec.reshape(n, -1), attn_out_norm[layer][None],
                      w_out[layer].astype(BF16), mix_post_norm[layer][None],
                      mlp_pre_norm[layer][None], w_ff1[layer].astype(BF16),
                      w_ff2[layer].astype(BF16), mlp_post_norm[layer][None])
    return x2.reshape(batch, seq, d)
```

```python
import functools

import numpy as np
import jax
import jax.numpy as jnp
from jax import lax
from jax.experimental import pallas as pl
from jax.experimental.pallas import tpu as pltpu

F32 = jnp.float32
BF16 = jnp.bfloat16

RMS_EPS = 1e-6
ATTN_HEAD_DIM = 64
ATTN_BLOCK = 128
DILATED_PATTERNS = ((128, 1), (512, 4), (2048, 16))
HGRN_HEAD_DIM = 128
HGRN_CHUNK = 64
HGRN_SEQ_TILE = 1024
LANES = 128
SUBLANES = 8
MASKED_SCORE = -1e30
VMEM_LIMIT_BYTES = 56 * 1024 * 1024
ROW_TILE = 512


def _rms(x, gain):
    return x * lax.rsqrt(jnp.mean(x * x, axis=-1, keepdims=True) + RMS_EPS) * gain


def _dot(a, b):
    return jnp.dot(a, b, preferred_element_type=F32)


def _dot_nt(a, b):
    return lax.dot_general(a, b, (((1,), (1,)), ((), ())), preferred_element_type=F32)


def _dot_tn(a, b):
    return lax.dot_general(a, b, (((0,), (0,)), ((), ())), preferred_element_type=F32)


def _inproj_kernel(x_ref, g_ref, w_ref, qkv_ref, hg_ref, *, attn_cols):
    h = _rms(x_ref[...], g_ref[...]).astype(BF16)
    qkv_ref[...] = _dot(h, w_ref[:, :attn_cols])
    hg_ref[...] = _dot(h, w_ref[:, attn_cols:])


def _inproj(x2, gain, w_in_b, attn_cols):
    n, d = x2.shape
    width = w_in_b.shape[1]
    return pl.pallas_call(
        functools.partial(_inproj_kernel, attn_cols=attn_cols),
        grid=(n // ROW_TILE,),
        in_specs=[
            pl.BlockSpec((ROW_TILE, d), lambda i: (i, 0)),
            pl.BlockSpec((1, d), lambda i: (0, 0)),
            pl.BlockSpec((d, width), lambda i: (0, 0), pipeline_mode=pl.Buffered(1)),
        ],
        out_specs=[
            pl.BlockSpec((ROW_TILE, attn_cols), lambda i: (i, 0)),
            pl.BlockSpec((ROW_TILE, width - attn_cols), lambda i: (i, 0)),
        ],
        out_shape=[
            jax.ShapeDtypeStruct((n, attn_cols), F32),
            jax.ShapeDtypeStruct((n, width - attn_cols), F32),
        ],
        compiler_params=pltpu.CompilerParams(
            dimension_semantics=("arbitrary",), vmem_limit_bytes=VMEM_LIMIT_BYTES),
        name="inproj",
    )(x2, gain, w_in_b)


def _attn_kernel(q_ref, k_ref, v_ref, o_ref, m_sc, acc_sc, den_sc, bias_sc, *, seq):
    hp = pl.program_id(1)
    blk = ATTN_BLOCK
    npat = len(DILATED_PATTERNS)
    lane = lax.broadcasted_iota(jnp.int32, (blk, LANES), 1)
    head0 = lane < ATTN_HEAD_DIM
    ones_tile = jnp.ones((blk, LANES), BF16)

    row = lax.broadcasted_iota(jnp.int32, (2 * blk, 2 * blk), 0)
    col = lax.broadcasted_iota(jnp.int32, (2 * blk, 2 * blk), 1)
    qi = row & (blk - 1)
    kj = col & (blk - 1)
    is_cur = col >= blk
    dist = jnp.where(is_cur, qi - kj, qi - kj + blk)
    valid = (dist >= 0) & (dist <= blk)
    slope = jnp.exp2((-(2 * hp + (row >> 7) + 1)).astype(F32))
    for bi, (_, dil) in enumerate(DILATED_PATTERNS):
        bias_sc[bi] = jnp.where(valid, -slope * (dist * dil).astype(F32), MASKED_SCORE)

    def pair(bi, dil, start, first):
        step = blk * dil

        def rows(st):
            return pl.ds(st, blk, stride=dil) if dil > 1 else pl.ds(st, blk)

        tile_rows = [None if first else rows(start - step), rows(start), rows(start + step)]
        keys = [None if r is None else k_ref[0, r, :].astype(BF16) for r in tile_rows]
        vals = [None if r is None else
                jnp.concatenate([v_ref[0, r, :].astype(BF16), ones_tile], axis=1)
                for r in tile_rows]
        scores = []
        for g in range(2):
            qb = q_ref[0, tile_rows[g + 1], :] * (ATTN_HEAD_DIM ** -0.5)
            qs = jnp.concatenate([jnp.where(head0, qb, 0.0), jnp.where(head0, 0.0, qb)],
                                 axis=0).astype(BF16)
            if keys[g] is None:
                s = _dot_nt(qs, keys[g + 1]) + bias_sc[bi, :, blk:]
            else:
                s = _dot_nt(qs, jnp.concatenate([keys[g], keys[g + 1]], axis=0)) + bias_sc[bi]
            scores.append(s)
        maxes = [jnp.max(s, axis=-1, keepdims=True) for s in scores]
        probs = [jnp.exp(s - m).astype(BF16) for s, m in zip(scores, maxes)]
        outs = []
        for g in range(2):
            if vals[g] is None:
                outs.append(_dot(probs[g], vals[g + 1]))
            else:
                outs.append(_dot(probs[g], jnp.concatenate([vals[g], vals[g + 1]], axis=0)))
        for g in range(2):
            out, mx, r = outs[g], maxes[g], tile_rows[g + 1]
            acc = jnp.where(head0, out[:blk, :LANES], out[blk:, :LANES])
            den = jnp.where(head0, out[:blk, LANES:], out[blk:, LANES:])
            mb = jnp.where(head0, jnp.broadcast_to(mx[:blk], (blk, LANES)),
                           jnp.broadcast_to(mx[blk:], (blk, LANES)))
            if bi > 0:
                m_old = m_sc[r, :]
                m_new = jnp.maximum(m_old, mb)
                w_old = jnp.exp(m_old - m_new)
                w_new = jnp.exp(mb - m_new)
                acc = acc_sc[r, :] * w_old + acc * w_new
                den = den_sc[r, :] * w_old + den * w_new
                mb = m_new
            if bi < npat - 1:
                m_sc[r, :] = mb
                acc_sc[r, :] = acc
                den_sc[r, :] = den
            else:
                o_ref[0, r, :] = acc / den

    for bi, (_, dil) in enumerate(DILATED_PATTERNS):
        npairs = seq // dil // blk // 2

        def residue(r, carry, bi=bi, dil=dil, npairs=npairs):
            pair(bi, dil, r, True)

            def body(i, c):
                pair(bi, dil, r + i * (2 * blk * dil), False)
                return c

            if npairs > 1:
                lax.fori_loop(1, npairs, body, 0)
            return carry

        lax.fori_loop(0, dil, residue, 0)


def _attention(qkv, batch, seq):
    width = qkv.shape[-1] // 3
    pairs = width // LANES
    blk = ATTN_BLOCK
    return pl.pallas_call(
        functools.partial(_attn_kernel, seq=seq),
        grid=(batch, pairs),
        in_specs=[
            pl.BlockSpec((1, seq, LANES), lambda b, h: (b, 0, h)),
            pl.BlockSpec((1, seq, LANES), lambda b, h: (b, 0, pairs + h)),
            pl.BlockSpec((1, seq, LANES), lambda b, h: (b, 0, 2 * pairs + h)),
        ],
        out_specs=pl.BlockSpec((1, seq, LANES), lambda b, h: (b, 0, h)),
        out_shape=jax.ShapeDtypeStruct((batch, seq, width), F32),
        scratch_shapes=[
            pltpu.VMEM((seq, LANES), F32),
            pltpu.VMEM((seq, LANES), F32),
            pltpu.VMEM((seq, LANES), F32),
            pltpu.VMEM((len(DILATED_PATTERNS), 2 * blk, 2 * blk), F32),
        ],
        compiler_params=pltpu.CompilerParams(
            dimension_semantics=("arbitrary", "arbitrary"), vmem_limit_bytes=VMEM_LIMIT_BYTES),
        name="dilated_attention",
    )(qkv, qkv, qkv)


def _hgrn_tables(chunk):
    levels = int(np.log2(chunk))
    t = np.arange(chunk)
    masks = np.zeros((levels + 1, chunk, chunk), np.float32)
    for l in range(levels):
        blk = t // (1 << l)
        masks[l] = ((blk[:, None] & 1) == 1) & (blk[None, :] == blk[:, None] - 1)
    masks[levels] = np.eye(chunk, dtype=np.float32)
    assert np.array_equal(masks.sum(0), np.tril(np.ones((chunk, chunk), np.float32)))
    return np.tril(np.ones((chunk, chunk), np.float32)), masks


def _hgrn_kernel(q_ref, f_ref, i_ref, gate_ref, lbl_ref, gain_ref, tri_ref, masks_ref,
                 o_ref, state_sc, b_sc, *, tile, chunk, layer, heads):
    hd = HGRN_HEAD_DIM
    levels = masks_ref.shape[0] - 1

    @pl.when(pl.program_id(1) == 0)
    def _():
        state_sc[...] = jnp.zeros(state_sc.shape, F32)

    logits = lbl_ref[...]
    e = jnp.exp(logits - jnp.max(logits, axis=0, keepdims=True))
    lb_all = jnp.sum(e[0:layer + 1, :], axis=0, keepdims=True) / jnp.sum(e, axis=0, keepdims=True)
    gain_all = gain_ref[...]
    row = lax.broadcasted_iota(jnp.int32, (chunk, hd), 0)
    odd_row = (row & 1) == 1
    low_half = (row & 4) == 0
    tri = tri_ref[...]

    def boundary(h, m):
        def bcast(r):
            return jnp.broadcast_to(b_sc[h, r:r + 1, :], (SUBLANES, hd))
        pieces = []
        for s in range(chunk // SUBLANES):
            if 2 * m >= SUBLANES:
                pieces.append(bcast((s * SUBLANES) // (2 * m) * (2 * m) + m - 1))
            else:
                pieces.append(jnp.where(low_half[:SUBLANES], bcast(s * SUBLANES + 1),
                                        bcast(s * SUBLANES + 5)))
        return jnp.concatenate(pieces, axis=0)

    def step(c, carry):
        rows = pl.ds(pl.multiple_of(c * chunk, chunk), chunk)
        qf, kf, fdec, v, b = [], [], [], [], []
        for h in range(heads):
            sl = slice(h * hd, (h + 1) * hd)
            lb = lb_all[:, sl]
            qr = q_ref[0, rows, sl]
            fr = f_ref[0, rows, sl]
            qf.append(qr * jax.nn.sigmoid(qr))
            f = lb + (1.0 - lb) * jax.nn.sigmoid(fr)
            kf.append((1.0 - lb) * jax.nn.sigmoid(-fr))
            fdec.append(f)
            v.append(i_ref[0, rows, sl].astype(BF16))
            g = jnp.log(f)
            g_hi = g.astype(BF16)
            r1 = g - g_hi.astype(F32)
            g_mid = r1.astype(BF16)
            g_lo = (r1 - g_mid.astype(F32)).astype(BF16)
            bh = _dot(tri, g_hi) + _dot(tri, g_mid) + _dot(tri, g_lo)
            b_sc[h] = bh
            b.append(bh)
        for h in range(heads):
            sl = slice(h * hd, (h + 1) * hd)
            a = masks_ref[levels] * _dot_nt(qf[h].astype(BF16), kf[h].astype(BF16))
            for l in range(levels):
                if l == 0:
                    dec = jnp.where(odd_row, fdec[h], 1.0)
                else:
                    dec = jnp.exp(-jnp.abs(b[h] - boundary(h, 1 << l)))
                a = a + masks_ref[l] * _dot_nt((qf[h] * dec).astype(BF16),
                                               (kf[h] * dec).astype(BF16))
            state_t = state_sc[h]
            o = (_dot(a.astype(BF16), v[h])
                 + _dot_nt((qf[h] * jnp.exp(b[h])).astype(BF16), state_t.astype(BF16)))
            b_last = b[h][chunk - 1:chunk, :]
            k_end = (kf[h] * jnp.exp(b_last - b[h])).astype(BF16)
            state_sc[h] = state_t * jnp.exp(b_last) + _dot_tn(v[h], k_end)
            gt = gate_ref[0, rows, sl]
            o_ref[0, rows, sl] = _rms(o, gain_all[:, sl]) * (gt * jax.nn.sigmoid(gt))
        return carry

    lax.fori_loop(0, tile // chunk, step, 0)


def _hgrn(hg, lb_logits, out_gain, batch, seq, layer):
    width = hg.shape[-1] // 4
    heads = width // HGRN_HEAD_DIM
    chunk = HGRN_CHUNK
    tile = HGRN_SEQ_TILE
    tri, masks = _hgrn_tables(chunk)
    nrows = lb_logits.shape[0]

    def col(k):
        return pl.BlockSpec((1, tile, width), lambda b, t, k=k: (b, t, k))

    return pl.pallas_call(
        functools.partial(_hgrn_kernel, tile=tile, chunk=chunk, layer=layer, heads=heads),
        grid=(batch, seq // tile),
        in_specs=[
            col(0), col(1), col(2), col(3),
            pl.BlockSpec((nrows, width), lambda b, t: (0, 0)),
            pl.BlockSpec((1, width), lambda b, t: (0, 0)),
            pl.BlockSpec(tri.shape, lambda b, t: (0, 0)),
            pl.BlockSpec(masks.shape, lambda b, t: (0, 0, 0)),
        ],
        out_specs=pl.BlockSpec((1, tile, width), lambda b, t: (b, t, 0)),
        out_shape=jax.ShapeDtypeStruct((batch, seq, width), F32),
        scratch_shapes=[
            pltpu.VMEM((heads, HGRN_HEAD_DIM, HGRN_HEAD_DIM), F32),
            pltpu.VMEM((heads, chunk, HGRN_HEAD_DIM), F32),
        ],
        compiler_params=pltpu.CompilerParams(
            dimension_semantics=("arbitrary", "arbitrary"), vmem_limit_bytes=VMEM_LIMIT_BYTES),
        name="hgrn2",
    )(hg, hg, hg, hg, lb_logits, out_gain, jnp.asarray(tri, BF16), jnp.asarray(masks, F32))


def _out_ffn_kernel(x_ref, a_ref, r_ref, ga_ref, wo_ref, gpost_ref, gpre_ref, w1_ref, w2_ref,
                    gpost2_ref, o_ref, *, ff_tile):
    aw = a_ref.shape[1]
    an = _rms(a_ref[...], ga_ref[...]).astype(BF16)
    mixed = _dot(an, wo_ref[:aw, :]) + _dot(r_ref[...].astype(BF16), wo_ref[aw:, :])
    x1 = x_ref[...] + _rms(mixed, gpost_ref[...])
    h = _rms(x1, gpre_ref[...]).astype(BF16)
    d_ff = w1_ref.shape[1]
    acc = jnp.zeros(x1.shape, F32)
    for j in range(d_ff // ff_tile):
        u = _dot(h, w1_ref[:, j * ff_tile:(j + 1) * ff_tile])
        u = jnp.square(jnp.maximum(u, 0.0)).astype(BF16)
        acc = acc + _dot(u, w2_ref[j * ff_tile:(j + 1) * ff_tile, :])
    o_ref[...] = x1 + _rms(acc, gpost2_ref[...])


def _out_ffn(x2, attn, rec, ga, wo_b, gpost, gpre, w1_b, w2_b, gpost2):
    n, d = x2.shape
    aw = attn.shape[1]
    rw = rec.shape[1]
    d_ff = w1_b.shape[1]

    def rows(w):
        return pl.BlockSpec((ROW_TILE, w), lambda i: (i, 0))

    def whole(shape):
        return pl.BlockSpec(shape, lambda i: (0, 0), pipeline_mode=pl.Buffered(1))

    return pl.pallas_call(
        functools.partial(_out_ffn_kernel, ff_tile=1024),
        grid=(n // ROW_TILE,),
        in_specs=[
            rows(d), rows(aw), rows(rw), whole((1, aw)), whole((aw + rw, d)), whole((1, d)),
            whole((1, d)), whole((d, d_ff)), whole((d_ff, d)), whole((1, d)),
        ],
        out_specs=rows(d),
        out_shape=jax.ShapeDtypeStruct((n, d), F32),
        compiler_params=pltpu.CompilerParams(
            dimension_semantics=("arbitrary",), vmem_limit_bytes=VMEM_LIMIT_BYTES),
        name="outproj_ffn",
    )(x2, attn, rec, ga, wo_b, gpost, gpre, w1_b, w2_b, gpost2)


def kernel(x, mix_pre_norm, w_in, attn_out_norm, hgrn_lb_logits, hgrn_out_norm, w_out,
           mix_post_norm, mlp_pre_norm, w_ff1, w_ff2, mlp_post_norm):
    batch, seq, d = x.shape
    depth = w_in.shape[0]
    aw = attn_out_norm.shape[1]
    n = batch * seq
    x2 = x.reshape(n, d)
    for layer in range(depth):
        qkv, hg = _inproj(x2, mix_pre_norm[layer][None], w_in[layer].astype(BF16), 3 * aw)
        attn = _attention(qkv.reshape(batch, seq, -1), batch, seq)
        rec = _hgrn(hg.reshape(batch, seq, -1), hgrn_lb_logits, hgrn_out_norm[layer][None],
                    batch, seq, layer)
        x2 = _out_ffn(x2, attn.reshape(n, -1), rec.reshape(n, -1), attn_out_norm[layer][None],
                      w_out[layer].astype(BF16), mix_post_norm[layer][None],
                      mlp_pre_norm[layer][None], w_ff1[layer].astype(BF16),
                      w_ff2[layer].astype(BF16), mlp_post_norm[layer][None])
    return x2.reshape(batch, seq, d)
```

```python
import functools

import numpy as np
import jax
import jax.numpy as jnp
from jax import lax
from jax.experimental import pallas as pl
from jax.experimental.pallas import tpu as pltpu

F32 = jnp.float32
BF16 = jnp.bfloat16

RMS_EPS = 1e-6
ATTN_HEAD_DIM = 64
ATTN_BLOCK = 128
ATTN_GROUP = 4
DILATED_PATTERNS = ((128, 1), (512, 4), (2048, 16))
HGRN_HEAD_DIM = 128
HGRN_CHUNK = 64
HGRN_SEQ_TILE = 1024
LANES = 128
SUBLANES = 8
MASKED_SCORE = -1e30
LOG2E = 1.4426950408889634
VMEM_LIMIT_BYTES = 56 * 1024 * 1024
ROW_TILE = 512


def _rms(x, gain):
    return x * lax.rsqrt(jnp.mean(x * x, axis=-1, keepdims=True) + RMS_EPS) * gain


def _dot(a, b):
    return jnp.dot(a, b, preferred_element_type=F32)


def _dot_nt(a, b):
    return lax.dot_general(a, b, (((1,), (1,)), ((), ())), preferred_element_type=F32)


def _dot_tn(a, b):
    return lax.dot_general(a, b, (((0,), (0,)), ((), ())), preferred_element_type=F32)


def _inproj_kernel(x_ref, g_ref, w_ref, qkv_ref, hg_ref, *, attn_cols):
    h = _rms(x_ref[...], g_ref[...]).astype(BF16)
    qkv_ref[...] = _dot(h, w_ref[:, :attn_cols])
    hg_ref[...] = _dot(h, w_ref[:, attn_cols:])


def _inproj(x2, gain, w_in_b, attn_cols):
    n, d = x2.shape
    width = w_in_b.shape[1]
    return pl.pallas_call(
        functools.partial(_inproj_kernel, attn_cols=attn_cols),
        grid=(n // ROW_TILE,),
        in_specs=[
            pl.BlockSpec((ROW_TILE, d), lambda i: (i, 0)),
            pl.BlockSpec((1, d), lambda i: (0, 0)),
            pl.BlockSpec((d, width), lambda i: (0, 0), pipeline_mode=pl.Buffered(1)),
        ],
        out_specs=[
            pl.BlockSpec((ROW_TILE, attn_cols), lambda i: (i, 0)),
            pl.BlockSpec((ROW_TILE, width - attn_cols), lambda i: (i, 0)),
        ],
        out_shape=[
            jax.ShapeDtypeStruct((n, attn_cols), F32),
            jax.ShapeDtypeStruct((n, width - attn_cols), F32),
        ],
        compiler_params=pltpu.CompilerParams(
            dimension_semantics=("arbitrary",), vmem_limit_bytes=VMEM_LIMIT_BYTES),
        name="inproj",
    )(x2, gain, w_in_b)


def _attn_kernel(q_ref, k_ref, v_ref, o_ref, m_sc, acc_sc, den_sc, bias_sc, *, seq):
    hp = pl.program_id(1)
    blk = ATTN_BLOCK
    patterns = sorted(DILATED_PATTERNS, key=lambda p: -p[1])
    npat = len(patterns)
    lane =lax.broadcasted_iota(jnp.int32, (blk, LANES), 1)
    head0 = lane < ATTN_HEAD_DIM
    ones_tile = jnp.ones((blk, LANES), BF16)

    row = lax.broadcasted_iota(jnp.int32, (2 * blk, 2 * blk), 0)
    col = lax.broadcasted_iota(jnp.int32, (2 * blk, 2 * blk), 1)
    qi = row & (blk - 1)
    kj = col & (blk - 1)
    is_cur = col >= blk
    dist = jnp.where(is_cur, qi - kj, qi - kj + blk)
    valid = (dist >= 0) & (dist <= blk)
    slope = jnp.exp2((-(2 * hp + (row >> 7) + 1)).astype(F32))
    for bi, (_, dil) in enumerate(patterns):
        bias = -(slope * LOG2E) * (dist * dil).astype(F32)
        bias_sc[bi] = jnp.where(valid, bias, MASKED_SCORE).astype(bias_sc.dtype)

    def group(bi, dil, runs):
        step = blk * dil

        def rows(st):
            return pl.ds(st, blk, stride=dil) if dil > 1 else pl.ds(st, blk)

        blocks = []
        for start, first, count in runs:
            tile_rows = [None if first else rows(start - step)]
            tile_rows += [rows(start + g * step) for g in range(count)]
            keys = [None if r is None else k_ref[0, r, :].astype(BF16) for r in tile_rows]
            vals = [None if r is None else
                    jnp.concatenate([v_ref[0, r, :].astype(BF16), ones_tile], axis=1)
                    for r in tile_rows]
            for g in range(count):
                blocks.append((tile_rows[g + 1], keys[g], keys[g + 1], vals[g], vals[g + 1]))
        scores = []
        for r, kp, kc, _, _ in blocks:
            qb = q_ref[0, r, :] * (ATTN_HEAD_DIM ** -0.5 * LOG2E)
            qs = jnp.concatenate([jnp.where(head0, qb, 0.0), jnp.where(head0, 0.0, qb)],
                                 axis=0).astype(BF16)
            if kp is None:
                scores.append(_dot_nt(qs, kc) + bias_sc[bi, :, blk:])
            else:
                scores.append(_dot_nt(qs, jnp.concatenate([kp, kc], axis=0)) + bias_sc[bi])
        maxes = [jnp.max(s, axis=-1, keepdims=True) for s in scores]
        probs = [jnp.exp2(s - m).astype(BF16) for s, m in zip(scores, maxes)]
        outs = []
        for (_, _, _, vp, vc), p in zip(blocks, probs):
            outs.append(_dot(p, vc) if vp is None else _dot(p, jnp.concatenate([vp, vc], axis=0)))
        for (r, _, _, _, _), out, mx in zip(blocks, outs, maxes):
            acc = jnp.where(head0, out[:blk, :LANES], out[blk:, :LANES])
            den = jnp.where(head0, out[:blk, LANES:], out[blk:, LANES:])
            mb = jnp.where(head0, jnp.broadcast_to(mx[:blk], (blk, LANES)),
                           jnp.broadcast_to(mx[blk:], (blk, LANES)))
            if bi > 0:
                m_old = m_sc[r, :]
                m_new = jnp.maximum(m_old, mb)
                w_old = jnp.exp2(m_old - m_new)
                w_new = jnp.exp2(mb - m_new)
                acc = acc_sc[r, :] * w_old + acc * w_new
                den = den_sc[r, :] * w_old + den * w_new
                mb = m_new
            if bi < npat - 1:
                m_sc[r, :] = mb
                acc_sc[r, :] = acc
                den_sc[r, :] = den
            else:
                o_ref[0, r, :] = acc / den

    for bi, (_, dil) in enumerate(patterns):
        run_len = seq // dil // blk
        if run_len >= ATTN_GROUP:
            ngroups = run_len // ATTN_GROUP

            def residue(r, carry, bi=bi, dil=dil, ngroups=ngroups):
                group(bi, dil, [(r, True, ATTN_GROUP)])

                def body(i, c):
                    group(bi, dil, [(r + i * (ATTN_GROUP * blk * dil), False, ATTN_GROUP)])
                    return c

                if ngroups > 1:
                    lax.fori_loop(1, ngroups, body, 0)
                return carry

            lax.fori_loop(0, dil, residue, 0)
        else:
            nruns = ATTN_GROUP // run_len

            def residues(i, carry, bi=bi, dil=dil, nruns=nruns, run_len=run_len):
                group(bi, dil, [(i * nruns + u, True, run_len) for u in range(nruns)])
                return carry

            lax.fori_loop(0, dil // nruns, residues, 0)


def _attention(qkv, batch, seq):
    width = qkv.shape[-1] // 3
    pairs = width // LANES
    blk = ATTN_BLOCK
    return pl.pallas_call(
        functools.partial(_attn_kernel, seq=seq),
        grid=(batch, pairs),
        in_specs=[
            pl.BlockSpec((1, seq, LANES), lambda b, h: (b, 0, h)),
            pl.BlockSpec((1, seq, LANES), lambda b, h: (b, 0, pairs + h)),
            pl.BlockSpec((1, seq, LANES), lambda b, h: (b, 0, 2 * pairs + h)),
        ],
        out_specs=pl.BlockSpec((1, seq, LANES), lambda b, h: (b, 0, h)),
        out_shape=jax.ShapeDtypeStruct((batch, seq, width), F32),
        scratch_shapes=[
            pltpu.VMEM((seq, LANES), F32),
            pltpu.VMEM((seq, LANES), F32),
            pltpu.VMEM((seq, LANES), F32),
            pltpu.VMEM((len(DILATED_PATTERNS), 2 * blk, 2 * blk), F32),
        ],
        compiler_params=pltpu.CompilerParams(
            dimension_semantics=("arbitrary", "arbitrary"), vmem_limit_bytes=VMEM_LIMIT_BYTES),
        name="dilated_attention",
    )(qkv, qkv, qkv)


def _hgrn_tables(chunk):
    levels = int(np.log2(chunk))
    t = np.arange(chunk)
    masks = np.zeros((levels + 1, chunk, chunk), np.float32)
    for l in range(levels):
        blk = t // (1 << l)
        masks[l] = ((blk[:, None] & 1) == 1) & (blk[None, :] == blk[:, None] - 1)
    masks[levels] = np.eye(chunk, dtype=np.float32)
    assert np.array_equal(masks.sum(0), np.tril(np.ones((chunk, chunk), np.float32)))
    return np.tril(np.ones((chunk, chunk), np.float32)), masks


def _hgrn_kernel(q_ref, f_ref, i_ref, gate_ref, lbl_ref, gain_ref, tri_ref, masks_ref,
                 o_ref, state_sc, b_sc, *, tile, chunk, layer, heads):
    hd = HGRN_HEAD_DIM
    levels = masks_ref.shape[0] - 1

    @pl.when(pl.program_id(1) == 0)
    def _():
        state_sc[...] = jnp.zeros(state_sc.shape, F32)

    logits = lbl_ref[...]
    e = jnp.exp(logits - jnp.max(logits, axis=0, keepdims=True))
    lb_all = jnp.sum(e[0:layer + 1, :], axis=0, keepdims=True) / jnp.sum(e, axis=0, keepdims=True)
    gain_all = gain_ref[...]
    row = lax.broadcasted_iota(jnp.int32, (chunk, hd), 0)
    odd_row = (row & 1) == 1
    low_half = (row & 4) == 0
    tri = tri_ref[...]

    def boundary(h, m):
        def bcast(r):
            return jnp.broadcast_to(b_sc[h, r:r + 1, :], (SUBLANES, hd))
        pieces = []
        for s in range(chunk // SUBLANES):
            if 2 * m >= SUBLANES:
                pieces.append(bcast((s * SUBLANES) // (2 * m) * (2 * m) + m - 1))
            else:
                pieces.append(jnp.where(low_half[:SUBLANES], bcast(s * SUBLANES + 1),
                                        bcast(s * SUBLANES + 5)))
        return jnp.concatenate(pieces, axis=0)

    def step(c, carry):
        rows = pl.ds(pl.multiple_of(c * chunk, chunk), chunk)
        qf, kf, fdec, v, b = [], [], [], [], []
        for h in range(heads):
            sl = slice(h * hd, (h + 1) * hd)
            lb = lb_all[:, sl]
            qr = q_ref[0, rows, sl]
            fr = f_ref[0, rows, sl]
            qf.append(qr * jax.nn.sigmoid(qr))
            f = lb + (1.0 - lb) * jax.nn.sigmoid(fr)
            kf.append((1.0 - lb) * jax.nn.sigmoid(-fr))
            fdec.append(f)
            v.append(i_ref[0, rows, sl].astype(BF16))
            g = jnp.log(f)
            g_hi = g.astype(BF16)
            r1 = g - g_hi.astype(F32)
            g_mid = r1.astype(BF16)
            g_lo = (r1 - g_mid.astype(F32)).astype(BF16)
            bh = _dot(tri, g_hi) + _dot(tri, g_mid) + _dot(tri, g_lo)
            b_sc[h] = bh
            b.append(bh)
        o_inter = []
        for h in range(heads):
            state_t = state_sc[h]
            o_inter.append(_dot_nt((qf[h] * jnp.exp(b[h])).astype(BF16), state_t.astype(BF16)))
            b_last = b[h][chunk - 1:chunk, :]
            k_end = (kf[h] * jnp.exp(b_last - b[h])).astype(BF16)
            state_sc[h] = state_t * jnp.exp(b_last) + _dot_tn(v[h], k_end)
        a = []
        for h in range(heads):
            ah = masks_ref[levels] * _dot_nt(qf[h].astype(BF16), kf[h].astype(BF16))
            for l in range(levels):
                if l == 0:
                    dec = jnp.where(odd_row, fdec[h], 1.0)
                else:
                    dec = jnp.exp(-jnp.abs(b[h] - boundary(h, 1 << l)))
                ah = ah + masks_ref[l] * _dot_nt((qf[h] * dec).astype(BF16),
                                                 (kf[h] * dec).astype(BF16))
            a.append(ah)
        for h in range(heads):
            sl = slice(h * hd, (h + 1) * hd)
            o = _dot(a[h].astype(BF16), v[h]) + o_inter[h]
            gt = gate_ref[0, rows, sl]
            o_ref[0, rows, sl] = _rms(o, gain_all[:, sl]) * (gt * jax.nn.sigmoid(gt))
        return carry

    lax.fori_loop(0, tile // chunk, step, 0)


def _hgrn(hg, lb_logits, out_gain, batch, seq, layer):
    width = hg.shape[-1] // 4
    heads = width // HGRN_HEAD_DIM
    chunk = HGRN_CHUNK
    tile = HGRN_SEQ_TILE
    tri, masks = _hgrn_tables(chunk)
    nrows = lb_logits.shape[0]

    def col(k):
        return pl.BlockSpec((1, tile, width), lambda b, t, k=k: (b, t, k))

    return pl.pallas_call(
        functools.partial(_hgrn_kernel, tile=tile, chunk=chunk, layer=layer, heads=heads),
        grid=(batch, seq // tile),
        in_specs=[
            col(0), col(1), col(2), col(3),
            pl.BlockSpec((nrows, width), lambda b, t: (0, 0)),
            pl.BlockSpec((1, width), lambda b, t: (0, 0)),
            pl.BlockSpec(tri.shape, lambda b, t: (0, 0)),
            pl.BlockSpec(masks.shape, lambda b, t: (0, 0, 0)),
        ],
        out_specs=pl.BlockSpec((1, tile, width), lambda b, t: (b, t, 0)),
        out_shape=jax.ShapeDtypeStruct((batch, seq, width), F32),
        scratch_shapes=[
            pltpu.VMEM((heads, HGRN_HEAD_DIM, HGRN_HEAD_DIM), F32),
            pltpu.VMEM((heads, chunk, HGRN_HEAD_DIM), F32),
        ],
        compiler_params=pltpu.CompilerParams(
            dimension_semantics=("arbitrary", "arbitrary"), vmem_limit_bytes=VMEM_LIMIT_BYTES),
        name="hgrn2",
    )(hg, hg, hg, hg, lb_logits, out_gain, jnp.asarray(tri, BF16), jnp.asarray(masks, F32))


def _out_ffn_kernel(x_ref, a_ref, r_ref, ga_ref, wo_ref, gpost_ref, gpre_ref, w1_ref, w2_ref,
                    gpost2_ref, o_ref, *, ff_tile):
    aw = a_ref.shape[1]
    an = _rms(a_ref[...], ga_ref[...]).astype(BF16)
    mixed = _dot(an, wo_ref[:aw, :]) + _dot(r_ref[...].astype(BF16), wo_ref[aw:, :])
    x1 = x_ref[...] + _rms(mixed, gpost_ref[...])
    h = _rms(x1, gpre_ref[...]).astype(BF16)
    d_ff = w1_ref.shape[1]
    acc = jnp.zeros(x1.shape, F32)
    for j in range(d_ff // ff_tile):
        u = _dot(h, w1_ref[:, j * ff_tile:(j + 1) * ff_tile])
        u = jnp.square(jnp.maximum(u, 0.0)).astype(BF16)
        acc = acc + _dot(u, w2_ref[j * ff_tile:(j + 1) * ff_tile, :])
    o_ref[...] = x1 + _rms(acc, gpost2_ref[...])


def _out_ffn(x2, attn, rec, ga, wo_b, gpost, gpre, w1_b, w2_b, gpost2):
    n, d = x2.shape
    aw = attn.shape[1]
    rw = rec.shape[1]
    d_ff = w1_b.shape[1]

    def rows(w):
        return pl.BlockSpec((ROW_TILE, w), lambda i: (i, 0))

    def whole(shape):
        return pl.BlockSpec(shape, lambda i: (0, 0), pipeline_mode=pl.Buffered(1))

    return pl.pallas_call(
        functools.partial(_out_ffn_kernel, ff_tile=1024),
        grid=(n // ROW_TILE,),
        in_specs=[
            rows(d), rows(aw), rows(rw), whole((1, aw)), whole((aw + rw, d)), whole((1, d)),
            whole((1, d)), whole((d, d_ff)), whole((d_ff, d)), whole((1, d)),
        ],
        out_specs=rows(d),
        out_shape=jax.ShapeDtypeStruct((n, d), F32),
        compiler_params=pltpu.CompilerParams(
            dimension_semantics=("arbitrary",), vmem_limit_bytes=VMEM_LIMIT_BYTES),
        name="outproj_ffn",
    )(x2, attn, rec, ga, wo_b, gpost, gpre, w1_b, w2_b, gpost2)


def kernel(x, mix_pre_norm, w_in, attn_out_norm, hgrn_lb_logits, hgrn_out_norm, w_out,
           mix_post_norm, mlp_pre_norm, w_ff1, w_ff2, mlp_post_norm):
    batch, seq, d = x.shape
    depth = w_in.shape[0]
    aw = attn_out_norm.shape[1]
    n = batch * seq
    x2 = x.reshape(n, d)
    for layer in range(depth):
        qkv, hg = _inproj(x2, mix_pre_norm[layer][None], w_in[layer].astype(BF16), 3 * aw)
        attn = _attention(qkv.reshape(batch, seq, -1), batch, seq)
        rec = _hgrn(hg.reshape(batch, seq, -1), hgrn_lb_logits, hgrn_out_norm[layer][None],
                    batch, seq, layer)
        x2 = _out_ffn(x2, attn.reshape(n, -1), rec.reshape(n, -1), attn_out_norm[layer][None],
                      w_out[layer].astype(BF16), mix_post_norm[layer][None],
                      mlp_pre_norm[layer][None], w_ff1[layer].astype(BF16),
                      w_ff2[layer].astype(BF16), mlp_post_norm[layer][None])
    return x2.reshape(batch, seq, d)
```

```python
import functools

import numpy as np
import jax
import jax.numpy as jnp
from jax import lax
from jax.experimental import pallas as pl
from jax.experimental.pallas import tpu as pltpu

F32 = jnp.float32
BF16 = jnp.bfloat16

RMS_EPS = 1e-6
ATTN_HEAD_DIM = 64
ATTN_BLOCK = 128
ATTN_GROUP = 4
DILATED_PATTERNS = ((128, 1), (512, 4), (2048, 16))
HGRN_HEAD_DIM = 128
HGRN_CHUNK = 64
HGRN_SEQ_TILE = 1024
LANES = 128
SUBLANES = 8
MASKED_SCORE = -1e30
LOG2E = 1.4426950408889634
VMEM_LIMIT_BYTES = 56 * 1024 * 1024
ROW_TILE = 512


def _rms(x, gain):
    return x * lax.rsqrt(jnp.mean(x * x, axis=-1, keepdims=True) + RMS_EPS) * gain


def _dot(a, b):
    return jnp.dot(a, b, preferred_element_type=F32)


def _dot_nt(a, b):
    return lax.dot_general(a, b, (((1,), (1,)), ((), ())), preferred_element_type=F32)


def _dot_tn(a, b):
    return lax.dot_general(a, b, (((0,), (0,)), ((), ())), preferred_element_type=F32)


def _inproj_kernel(x_ref, g_ref, w_ref, qkv_ref, hg_ref, *, attn_cols):
    h = _rms(x_ref[...], g_ref[...]).astype(BF16)
    qkv_ref[...] = _dot(h, w_ref[:, :attn_cols])
    hg_ref[...] = _dot(h, w_ref[:, attn_cols:])


def _inproj(x2, gain, w_in_b, attn_cols):
    n, d = x2.shape
    width = w_in_b.shape[1]
    return pl.pallas_call(
        functools.partial(_inproj_kernel, attn_cols=attn_cols),
        grid=(n // ROW_TILE,),
        in_specs=[
            pl.BlockSpec((ROW_TILE, d), lambda i: (i, 0)),
            pl.BlockSpec((1, d), lambda i: (0, 0)),
            pl.BlockSpec((d, width), lambda i: (0, 0), pipeline_mode=pl.Buffered(1)),
        ],
        out_specs=[
            pl.BlockSpec((ROW_TILE, attn_cols), lambda i: (i, 0)),
            pl.BlockSpec((ROW_TILE, width - attn_cols), lambda i: (i, 0)),
        ],
        out_shape=[
            jax.ShapeDtypeStruct((n, attn_cols), F32),
            jax.ShapeDtypeStruct((n, width - attn_cols), F32),
        ],
        compiler_params=pltpu.CompilerParams(
            dimension_semantics=("arbitrary",), vmem_limit_bytes=VMEM_LIMIT_BYTES),
        name="inproj",
    )(x2, gain, w_in_b)


def _attn_kernel(q_ref, k_ref, v_ref, o_ref, m_sc, acc_sc, den_sc, bias_sc,
                 p0_sc, mb0_sc, p1_sc, mb1_sc, *, seq):
    hp = pl.program_id(1)
    blk = ATTN_BLOCK
    patterns = sorted(DILATED_PATTERNS, key=lambda p: -p[1])
    npat = len(patterns)
    lane =lax.broadcasted_iota(jnp.int32, (blk, LANES), 1)
    head0 = lane < ATTN_HEAD_DIM
    ones_tile = jnp.ones((blk, LANES), BF16)

    row = lax.broadcasted_iota(jnp.int32, (2 * blk, 2 * blk), 0)
    col = lax.broadcasted_iota(jnp.int32, (2 * blk, 2 * blk), 1)
    qi = row & (blk - 1)
    kj = col & (blk - 1)
    is_cur = col >= blk
    dist = jnp.where(is_cur, qi - kj, qi - kj + blk)
    valid = (dist >= 0) & (dist <= blk)
    slope = jnp.exp2((-(2 * hp + (row >> 7) + 1)).astype(F32))
    for bi, (_, dil) in enumerate(patterns):
        bias = -(slope * LOG2E) * (dist * dil).astype(F32)
        bias_sc[2 * bi] = jnp.where(valid, bias, MASKED_SCORE)
        bias_sc[2 * bi + 1] = jnp.where(valid & is_cur, bias, MASKED_SCORE)
    slots = ((p0_sc, mb0_sc), (p1_sc, mb1_sc))

    def tile_rows_of(dil, run):
        start, first, count = run
        step = blk * dil

        def rows(st):
            return pl.ds(st, blk, stride=dil) if dil > 1 else pl.ds(st, blk)

        if first is True:
            prev = None
        elif first is False:
            prev = rows(start - step)
        else:
            prev = rows(jnp.where(first, start, start - step))
        return [prev] + [rows(start + g * step) for g in range(count)]

    def scores_stage(bi, dil, runs, slot):
        p_sc, mb_sc = slots[slot]
        blocks = []
        for run in runs:
            tile_rows = tile_rows_of(dil, run)
            first = run[1]
            keys = [None if r is None else k_ref[0, r, :].astype(BF16) for r in tile_rows]
            for g in range(run[2]):
                dynamic_first = g == 0 and not isinstance(first, bool)
                masked_prev = first.astype(jnp.int32) if dynamic_first else 0
                blocks.append((tile_rows[g + 1], keys[g], keys[g + 1], 2 * bi + masked_prev))
        scores = []
        for r, kp, kc, table in blocks:
            qb = q_ref[0, r, :] * (ATTN_HEAD_DIM ** -0.5 * LOG2E)
            qs = jnp.concatenate([jnp.where(head0, qb, 0.0), jnp.where(head0, 0.0, qb)],
                                 axis=0).astype(BF16)
            if kp is None:
                scores.append(_dot_nt(qs, kc) + bias_sc[table, :, blk:])
            else:
                scores.append(_dot_nt(qs, jnp.concatenate([kp, kc], axis=0)) + bias_sc[table])
        maxes = [jnp.max(s, axis=-1, keepdims=True) for s in scores]
        for idx, (s, mx) in enumerate(zip(scores, maxes)):
            p_sc[idx, :, :s.shape[1]] = jnp.exp2(s - mx).astype(BF16)
            mb_sc[idx] = jnp.where(head0, jnp.broadcast_to(mx[:blk], (blk, LANES)),
                                   jnp.broadcast_to(mx[blk:], (blk, LANES)))

    def values_stage(bi, dil, runs, slot):
        p_sc, mb_sc = slots[slot]
        blocks = []
        for run in runs:
            tile_rows = tile_rows_of(dil, run)
            vals = [None if r is None else
                    jnp.concatenate([v_ref[0, r, :].astype(BF16), ones_tile], axis=1)
                    for r in tile_rows]
            for g in range(run[2]):
                blocks.append((tile_rows[g + 1], vals[g], vals[g + 1]))
        outs = []
        for idx, (_, vp, vc) in enumerate(blocks):
            if vp is None:
                outs.append(_dot(p_sc[idx, :, :blk], vc))
            else:
                outs.append(_dot(p_sc[idx], jnp.concatenate([vp, vc], axis=0)))
        for idx, ((r, _, _), out) in enumerate(zip(blocks, outs)):
            acc = jnp.where(head0, out[:blk, :LANES], out[blk:, :LANES])
            den = jnp.where(head0, out[:blk, LANES:], out[blk:, LANES:])
            mb = mb_sc[idx]
            if bi > 0:
                m_old = m_sc[r, :]
                m_new = jnp.maximum(m_old, mb)
                w_old = jnp.exp2(m_old - m_new)
                w_new = jnp.exp2(mb - m_new)
                acc = acc_sc[r, :] * w_old + acc * w_new
                den = den_sc[r, :] * w_old + den * w_new
                mb = m_new
            if bi < npat - 1:
                m_sc[r, :] = mb
                acc_sc[r, :] = acc
                den_sc[r, :] = den
            else:
                o_ref[0, r, :] = acc / den

    ngroups = seq // blk // ATTN_GROUP
    assert ngroups % 2 == 0 and ngroups >= 4

    def runs_of(dil, g):
        run_len = seq // dil // blk
        if run_len >= ATTN_GROUP:
            per_residue = run_len // ATTN_GROUP
            i = g % per_residue
            start = g // per_residue + i * (ATTN_GROUP * blk * dil)
            return [(start, jnp.asarray(i == 0), ATTN_GROUP)]
        nruns = ATTN_GROUP // run_len
        return [(g * nruns + u, True, run_len) for u in range(nruns)]

    scores_stage(0, patterns[0][1], runs_of(patterns[0][1], 0), 0)
    for bi, (_, dil) in enumerate(patterns):

        def steady(j, carry, bi=bi, dil=dil):
            g = 2 * j
            scores_stage(bi, dil, runs_of(dil, g + 1), 1)
            values_stage(bi, dil, runs_of(dil, g), 0)
            scores_stage(bi, dil, runs_of(dil, g + 2), 0)
            values_stage(bi, dil, runs_of(dil, g + 1), 1)
            return carry

        lax.fori_loop(0, (ngroups - 2) // 2, steady, 0)
        scores_stage(bi, dil, runs_of(dil, ngroups - 1), 1)
        values_stage(bi, dil, runs_of(dil, ngroups - 2), 0)
        if bi + 1 < npat:
            next_dil = patterns[bi + 1][1]
            scores_stage(bi + 1, next_dil, runs_of(next_dil, 0), 0)
        values_stage(bi, dil, runs_of(dil, ngroups - 1), 1)


def _attention(qkv, batch, seq):
    width = qkv.shape[-1] // 3
    pairs = width // LANES
    blk = ATTN_BLOCK
    return pl.pallas_call(
        functools.partial(_attn_kernel, seq=seq),
        grid=(batch, pairs),
        in_specs=[
            pl.BlockSpec((1, seq, LANES), lambda b, h: (b, 0, h)),
            pl.BlockSpec((1, seq, LANES), lambda b, h: (b, 0, pairs + h)),
            pl.BlockSpec((1, seq, LANES), lambda b, h: (b, 0, 2 * pairs + h)),
        ],
        out_specs=pl.BlockSpec((1, seq, LANES), lambda b, h: (b, 0, h)),
        out_shape=jax.ShapeDtypeStruct((batch, seq, width), F32),
        scratch_shapes=[
            pltpu.VMEM((seq, LANES), F32),
            pltpu.VMEM((seq, LANES), F32),
            pltpu.VMEM((seq, LANES), F32),
            pltpu.VMEM((2 * len(DILATED_PATTERNS), 2 * blk, 2 * blk), F32),
        ] + 2 * [pltpu.VMEM((ATTN_GROUP, 2 * blk, 2 * blk), BF16),
                 pltpu.VMEM((ATTN_GROUP, blk, LANES), F32)],
        compiler_params=pltpu.CompilerParams(
            dimension_semantics=("arbitrary", "arbitrary"), vmem_limit_bytes=VMEM_LIMIT_BYTES),
        name="dilated_attention",
    )(qkv, qkv, qkv)


def _hgrn_tables(chunk):
    levels = int(np.log2(chunk))
    t = np.arange(chunk)
    masks = np.zeros((levels + 1, chunk, chunk), np.float32)
    for l in range(levels):
        blk = t // (1 << l)
        masks[l] = ((blk[:, None] & 1) == 1) & (blk[None, :] == blk[:, None] - 1)
    masks[levels] = np.eye(chunk, dtype=np.float32)
    assert np.array_equal(masks.sum(0), np.tril(np.ones((chunk, chunk), np.float32)))
    return np.tril(np.ones((chunk, chunk), np.float32)), masks


def _hgrn_kernel(q_ref, f_ref, i_ref, gate_ref, lbl_ref, gain_ref, tri_ref, masks_ref,
                 o_ref, state_sc, *slot_refs, tile, chunk, layer, heads):
    hd = HGRN_HEAD_DIM
    levels = masks_ref.shape[0] - 1
    slots = (slot_refs[:4], slot_refs[4:])
    nchunks = tile // chunk
    assert nchunks % 2 == 0 and nchunks >= 4

    @pl.when(pl.program_id(1) == 0)
    def _():
        state_sc[...] = jnp.zeros(state_sc.shape, F32)

    logits = lbl_ref[...]
    e = jnp.exp(logits - jnp.max(logits, axis=0, keepdims=True))
    lb_all = jnp.sum(e[0:layer + 1, :], axis=0, keepdims=True) / jnp.sum(e, axis=0, keepdims=True)
    gain_all = gain_ref[...]
    row = lax.broadcasted_iota(jnp.int32, (chunk, hd), 0)
    odd_row = (row & 1) == 1
    low_half = (row & 4) == 0
    tri = tri_ref[...]

    def chunk_rows(c):
        return pl.ds(pl.multiple_of(c * chunk, chunk), chunk)

    role_sign = [jnp.where(((row >> l) & 1) == 1, 1.0, -1.0) for l in range(levels)]

    def boundary(b_sc, h, m):
        def bcast(r):
            return jnp.broadcast_to(b_sc[h, r:r + 1, :], (SUBLANES, hd))
        pieces = []
        for s in range(chunk // SUBLANES):
            if 2 * m >= SUBLANES:
                pieces.append(bcast((s * SUBLANES) // (2 * m) * (2 * m) + m - 1))
            else:
                pieces.append(jnp.where(low_half[:SUBLANES], bcast(s * SUBLANES + 1),
                                        bcast(s * SUBLANES + 5)))
        return jnp.concatenate(pieces, axis=0)

    def prepare(c, slot):
        qk_sc, aux_sc, b_sc, _ = slots[slot]
        rows = chunk_rows(c)
        for h in range(heads):
            sl = slice(h * hd, (h + 1) * hd)
            lb = lb_all[:, sl]
            qr = q_ref[0, rows, sl]
            sig = jax.nn.sigmoid(f_ref[0, rows, sl])
            f = lb + (1.0 - lb) * sig
            qk_sc[h] = jnp.concatenate([qr * jax.nn.sigmoid(qr), (1.0 - lb) * (1.0 - sig)],
                                       axis=0).astype(BF16)
            aux_sc[h] = jnp.concatenate([jnp.where(odd_row, f, 1.0), i_ref[0, rows, sl]],
                                        axis=0).astype(BF16)
            g = jnp.log2(f)
            g_hi = g.astype(BF16)
            r1 = g - g_hi.astype(F32)
            g_mid = r1.astype(BF16)
            g_lo = (r1 - g_mid.astype(F32)).astype(BF16)
            b_sc[h] = _dot(tri, g_hi) + _dot(tri, g_mid) + _dot(tri, g_lo)

    def mix(slot):
        qk_sc, aux_sc, b_sc, out_sc = slots[slot]
        o_inter = []
        for h in range(heads):
            b = b_sc[h]
            state_t = state_sc[h]
            q_dec = qk_sc[h, :chunk, :] * jnp.exp2(b).astype(BF16)
            o_inter.append(_dot_nt(q_dec, state_t.astype(BF16)))
            b_last = b[chunk - 1:chunk, :]
            k_end = qk_sc[h, chunk:, :] * jnp.exp2(b_last - b).astype(BF16)
            state_sc[h] = state_t * jnp.exp2(b_last) + _dot_tn(aux_sc[h, chunk:, :], k_end)
        a = []
        for h in range(heads):
            qk = qk_sc[h]
            ah = masks_ref[levels] * _dot_nt(qk[:chunk], qk[chunk:])
            for l in range(levels):
                if l == 0:
                    dec = aux_sc[h, :chunk, :]
                else:
                    expo = (b_sc[h] - boundary(b_sc, h, 1 << l)) * role_sign[l]
                    dec = jnp.exp2(expo).astype(BF16)
                prod = qk * jnp.concatenate([dec, dec], axis=0)
                ah = ah + masks_ref[l] * _dot_nt(prod[:chunk], prod[chunk:])
            a.append(ah)
        for h in range(heads):
            out_sc[h] = _dot(a[h].astype(BF16), aux_sc[h, chunk:, :]) + o_inter[h]

    def emit(c, slot):
        out_sc = slots[slot][3]
        rows = chunk_rows(c)
        for h in range(heads):
            sl = slice(h * hd, (h + 1) * hd)
            gt = gate_ref[0, rows, sl]
            o_ref[0, rows, sl] = _rms(out_sc[h], gain_all[:, sl]) * (gt * jax.nn.sigmoid(gt))

    prepare(0, 0)
    prepare(1, 1)
    mix(0)

    def steady(j, carry):
        c = 2 * j + 1
        emit(c - 1, 0)
        prepare(c + 1, 0)
        mix(1)
        emit(c, 1)
        prepare(c + 2, 1)
        mix(0)
        return carry

    lax.fori_loop(0, (nchunks - 2) // 2, steady, 0)
    emit(nchunks - 2, 0)
    mix(1)
    emit(nchunks - 1, 1)


def _hgrn(hg, lb_logits, out_gain, batch, seq, layer):
    width = hg.shape[-1] // 4
    heads = width // HGRN_HEAD_DIM
    chunk = HGRN_CHUNK
    tile = HGRN_SEQ_TILE
    tri, masks = _hgrn_tables(chunk)
    nrows = lb_logits.shape[0]

    def col(k):
        return pl.BlockSpec((1, tile, width), lambda b, t, k=k: (b, t, k))

    return pl.pallas_call(
        functools.partial(_hgrn_kernel, tile=tile, chunk=chunk, layer=layer, heads=heads),
        grid=(batch, seq // tile),
        in_specs=[
            col(0), col(1), col(2), col(3),
            pl.BlockSpec((nrows, width), lambda b, t: (0, 0)),
            pl.BlockSpec((1, width), lambda b, t: (0, 0)),
            pl.BlockSpec(tri.shape, lambda b, t: (0, 0)),
            pl.BlockSpec(masks.shape, lambda b, t: (0, 0, 0)),
        ],
        out_specs=pl.BlockSpec((1, tile, width), lambda b, t: (b, t, 0)),
        out_shape=jax.ShapeDtypeStruct((batch, seq, width), F32),
        scratch_shapes=(
            [pltpu.VMEM((heads, HGRN_HEAD_DIM, HGRN_HEAD_DIM), F32)]
            + 2 * [pltpu.VMEM((heads, 2 * chunk, HGRN_HEAD_DIM), BF16),
                   pltpu.VMEM((heads, 2 * chunk, HGRN_HEAD_DIM), BF16),
                   pltpu.VMEM((heads, chunk, HGRN_HEAD_DIM), F32),
                   pltpu.VMEM((heads, chunk, HGRN_HEAD_DIM), F32)]),
        compiler_params=pltpu.CompilerParams(
            dimension_semantics=("arbitrary", "arbitrary"), vmem_limit_bytes=VMEM_LIMIT_BYTES),
        name="hgrn2",
    )(hg, hg, hg, hg, lb_logits, out_gain, jnp.asarray(tri, BF16), jnp.asarray(masks, F32))


def _out_ffn_kernel(x_ref, a_ref, r_ref, ga_ref, wo_ref, gpost_ref, gpre_ref, w1_ref, w2_ref,
                    gpost2_ref, o_ref, *, ff_tile):
    aw = a_ref.shape[1]
    an = _rms(a_ref[...], ga_ref[...]).astype(BF16)
    mixed = _dot(an, wo_ref[:aw, :]) + _dot(r_ref[...].astype(BF16), wo_ref[aw:, :])
    x1 = x_ref[...] + _rms(mixed, gpost_ref[...])
    h = _rms(x1, gpre_ref[...]).astype(BF16)
    d_ff = w1_ref.shape[1]
    acc = jnp.zeros(x1.shape, F32)
    for j in range(d_ff // ff_tile):
        u = _dot(h, w1_ref[:, j * ff_tile:(j + 1) * ff_tile])
        u = jnp.square(jnp.maximum(u, 0.0)).astype(BF16)
        acc = acc + _dot(u, w2_ref[j * ff_tile:(j + 1) * ff_tile, :])
    o_ref[...] = x1 + _rms(acc, gpost2_ref[...])


def _out_ffn(x2, attn, rec, ga, wo_b, gpost, gpre, w1_b, w2_b, gpost2):
    n, d = x2.shape
    aw = attn.shape[1]
    rw = rec.shape[1]
    d_ff = w1_b.shape[1]

    def rows(w):
        return pl.BlockSpec((ROW_TILE, w), lambda i: (i, 0))

    def whole(shape):
        return pl.BlockSpec(shape, lambda i: (0, 0), pipeline_mode=pl.Buffered(1))

    return pl.pallas_call(
        functools.partial(_out_ffn_kernel, ff_tile=1024),
        grid=(n // ROW_TILE,),
        in_specs=[
            rows(d), rows(aw), rows(rw), whole((1, aw)), whole((aw + rw, d)), whole((1, d)),
            whole((1, d)), whole((d, d_ff)), whole((d_ff, d)), whole((1, d)),
        ],
        out_specs=rows(d),
        out_shape=jax.ShapeDtypeStruct((n, d), F32),
        compiler_params=pltpu.CompilerParams(
            dimension_semantics=("arbitrary",), vmem_limit_bytes=VMEM_LIMIT_BYTES),
        name="outproj_ffn",
    )(x2, attn, rec, ga, wo_b, gpost, gpre, w1_b, w2_b, gpost2)


def kernel(x, mix_pre_norm, w_in, attn_out_norm, hgrn_lb_logits, hgrn_out_norm, w_out,
           mix_post_norm, mlp_pre_norm, w_ff1, w_ff2, mlp_post_norm):
    batch, seq, d = x.shape
    depth = w_in.shape[0]
    aw = attn_out_norm.shape[1]
    n = batch * seq
    x2 = x.reshape(n, d)
    for layer in range(depth):
        qkv, hg = _inproj(x2, mix_pre_norm[layer][None], w_in[layer].astype(BF16), 3 * aw)
        attn = _attention(qkv.reshape(batch, seq, -1), batch, seq)
        rec = _hgrn(hg.reshape(batch, seq, -1), hgrn_lb_logits, hgrn_out_norm[layer][None],
                    batch, seq, layer)
        x2 = _out_ffn(x2, attn.reshape(n, -1), rec.reshape(n, -1), attn_out_norm[layer][None],
                      w_out[layer].astype(BF16), mix_post_norm[layer][None],
                      mlp_pre_norm[layer][None], w_ff1[layer].astype(BF16),
                      w_ff2[layer].astype(BF16), mlp_post_norm[layer][None])
    return x2.reshape(batch, seq, d)
```

```python
import functools

import numpy as np
import jax
import jax.numpy as jnp
from jax import lax
from jax.experimental import pallas as pl
from jax.experimental.pallas import tpu as pltpu

F32 = jnp.float32
BF16 = jnp.bfloat16

RMS_EPS = 1e-6
ATTN_HEAD_DIM = 64
ATTN_BLOCK = 128
ATTN_GROUP = 4
DILATED_PATTERNS = ((128, 1), (512, 4), (2048, 16))
HGRN_HEAD_DIM = 128
HGRN_CHUNK = 64
HGRN_SEQ_TILE = 1024
LANES = 128
SUBLANES = 8
MASKED_SCORE = -1e30
LOG2E = 1.4426950408889634
VMEM_LIMIT_BYTES = 56 * 1024 * 1024
INPROJ_ROW_TILE = 1024
INPROJ_ROW_PARTS = 4
FFN_ROW_TILE = 1024
FFN_HIDDEN_TILE = 1024
FFN_ROW_PARTS = 4


def _rms(x, gain):
    return x * lax.rsqrt(jnp.mean(x * x, axis=-1, keepdims=True) + RMS_EPS) * gain


def _dot(a, b):
    return jnp.dot(a, b, preferred_element_type=F32)


def _dot_nt(a, b):
    return lax.dot_general(a, b, (((1,), (1,)), ((), ())), preferred_element_type=F32)


def _dot_tn(a, b):
    return lax.dot_general(a, b, (((0,), (0,)), ((), ())), preferred_element_type=F32)


def _inproj_kernel(x_ref, g_ref, w_ref, qkv_ref, hg_ref, *, attn_cols):
    part = x_ref.shape[0] // INPROJ_ROW_PARTS
    parts = [pl.ds(i * part, part) for i in range(INPROJ_ROW_PARTS)]
    normed = [_rms(x_ref[r, :], g_ref[...]).astype(BF16) for r in parts]
    for r, h in zip(parts, normed):
        qkv_ref[r, :] = _dot(h, w_ref[:, :attn_cols])
        hg_ref[r, :] = _dot(h, w_ref[:, attn_cols:])


def _inproj(x2, gain, w_in_b, attn_cols):
    n, d = x2.shape
    width = w_in_b.shape[1]
    return pl.pallas_call(
        functools.partial(_inproj_kernel, attn_cols=attn_cols),
        grid=(n // INPROJ_ROW_TILE,),
        in_specs=[
            pl.BlockSpec((INPROJ_ROW_TILE, d), lambda i: (i, 0)),
            pl.BlockSpec((1, d), lambda i: (0, 0)),
            pl.BlockSpec((d, width), lambda i: (0, 0), pipeline_mode=pl.Buffered(1)),
        ],
        out_specs=[
            pl.BlockSpec((INPROJ_ROW_TILE, attn_cols), lambda i: (i, 0)),
            pl.BlockSpec((INPROJ_ROW_TILE, width - attn_cols), lambda i: (i, 0)),
        ],
        out_shape=[
            jax.ShapeDtypeStruct((n, attn_cols), F32),
            jax.ShapeDtypeStruct((n, width - attn_cols), F32),
        ],
        compiler_params=pltpu.CompilerParams(
            dimension_semantics=("arbitrary",), vmem_limit_bytes=VMEM_LIMIT_BYTES),
        name="inproj",
    )(x2, gain, w_in_b)


def _attn_kernel(q_ref, k_ref, v_ref, o_ref, m_sc, acc_sc, den_sc, bias_sc,
                 p0_sc, mb0_sc, p1_sc, mb1_sc, *, seq):
    hp = pl.program_id(1)
    blk = ATTN_BLOCK
    patterns = sorted(DILATED_PATTERNS, key=lambda p: -p[1])
    npat = len(patterns)
    lane =lax.broadcasted_iota(jnp.int32, (blk, LANES), 1)
    head0 = lane < ATTN_HEAD_DIM
    ones_tile = jnp.ones((blk, LANES), BF16)

    row = lax.broadcasted_iota(jnp.int32, (2 * blk, 2 * blk), 0)
    col = lax.broadcasted_iota(jnp.int32, (2 * blk, 2 * blk), 1)
    qi = row & (blk - 1)
    kj = col & (blk - 1)
    is_cur = col >= blk
    dist = jnp.where(is_cur, qi - kj, qi - kj + blk)
    valid = (dist >= 0) & (dist <= blk)
    slope = jnp.exp2((-(2 * hp + (row >> 7) + 1)).astype(F32))
    for bi, (_, dil) in enumerate(patterns):
        bias = -(slope * LOG2E) * (dist * dil).astype(F32)
        bias_sc[2 * bi] = jnp.where(valid, bias, MASKED_SCORE).astype(BF16)
        bias_sc[2 * bi + 1] = jnp.where(valid & is_cur, bias, MASKED_SCORE).astype(BF16)
    slots = ((p0_sc, mb0_sc), (p1_sc, mb1_sc))

    def tile_rows_of(dil, run):
        start, first, count = run
        step = blk * dil

        def rows(st):
            return pl.ds(st, blk, stride=dil) if dil > 1 else pl.ds(st, blk)

        if first is True:
            prev = None
        elif first is False:
            prev = rows(start - step)
        else:
            prev = rows(jnp.where(first, start, start - step))
        return [prev] + [rows(start + g * step) for g in range(count)]

    def scores_stage(bi, dil, runs, slot):
        p_sc, mb_sc = slots[slot]
        blocks = []
        for run in runs:
            tile_rows = tile_rows_of(dil, run)
            first = run[1]
            keys = [None if r is None else k_ref[0, r, :].astype(BF16) for r in tile_rows]
            for g in range(run[2]):
                dynamic_first = g == 0 and not isinstance(first, bool)
                masked_prev = first.astype(jnp.int32) if dynamic_first else 0
                blocks.append((tile_rows[g + 1], keys[g], keys[g + 1], 2 * bi + masked_prev))
        scores = []
        for r, kp, kc, table in blocks:
            qb = q_ref[0, r, :] * (ATTN_HEAD_DIM ** -0.5 * LOG2E)
            qs = jnp.concatenate([jnp.where(head0, qb, 0.0), jnp.where(head0, 0.0, qb)],
                                 axis=0).astype(BF16)
            if kp is None:
                scores.append(_dot_nt(qs, kc).astype(BF16) + bias_sc[table, :, blk:])
            else:
                kcat = jnp.concatenate([kp, kc], axis=0)
                scores.append(_dot_nt(qs, kcat).astype(BF16) + bias_sc[table])
        maxes = [jnp.max(s, axis=-1, keepdims=True) for s in scores]
        for idx, (s, mx) in enumerate(zip(scores, maxes)):
            p_sc[idx, :, :s.shape[1]] = jnp.exp2(s - mx)
            mx = mx.astype(F32)
            mb_sc[idx] = jnp.where(head0, jnp.broadcast_to(mx[:blk], (blk, LANES)),
                                   jnp.broadcast_to(mx[blk:], (blk, LANES)))

    def values_stage(bi, dil, runs, slot):
        p_sc, mb_sc = slots[slot]
        blocks = []
        for run in runs:
            tile_rows = tile_rows_of(dil, run)
            vals = [None if r is None else
                    jnp.concatenate([v_ref[0, r, :].astype(BF16), ones_tile], axis=1)
                    for r in tile_rows]
            for g in range(run[2]):
                blocks.append((tile_rows[g + 1], vals[g], vals[g + 1]))
        outs = []
        for idx, (_, vp, vc) in enumerate(blocks):
            if vp is None:
                outs.append(_dot(p_sc[idx, :, :blk], vc))
            else:
                outs.append(_dot(p_sc[idx], jnp.concatenate([vp, vc], axis=0)))
        for idx, ((r, _, _), out) in enumerate(zip(blocks, outs)):
            acc = jnp.where(head0, out[:blk, :LANES], out[blk:, :LANES])
            den = jnp.where(head0, out[:blk, LANES:], out[blk:, LANES:])
            mb = mb_sc[idx]
            if bi > 0:
                m_old = m_sc[r, :]
                m_new = jnp.maximum(m_old, mb)
                w_old = jnp.exp2(m_old - m_new)
                w_new = jnp.exp2(mb - m_new)
                acc = acc_sc[r, :] * w_old + acc * w_new
                den = den_sc[r, :] * w_old + den * w_new
                mb = m_new
            if bi < npat - 1:
                m_sc[r, :] = mb
                acc_sc[r, :] = acc
                den_sc[r, :] = den
            else:
                o_ref[0, r, :] = acc / den

    ngroups = seq // blk // ATTN_GROUP
    assert ngroups % 2 == 0 and ngroups >= 4

    def runs_of(dil, g):
        run_len = seq // dil // blk
        if run_len >= ATTN_GROUP:
            per_residue = run_len // ATTN_GROUP
            i = g % per_residue
            start = g // per_residue + i * (ATTN_GROUP * blk * dil)
            return [(start, jnp.asarray(i == 0), ATTN_GROUP)]
        nruns = ATTN_GROUP // run_len
        return [(g * nruns + u, True, run_len) for u in range(nruns)]

    scores_stage(0, patterns[0][1], runs_of(patterns[0][1], 0), 0)
    for bi, (_, dil) in enumerate(patterns):

        def steady(j, carry, bi=bi, dil=dil):
            g = 2 * j
            scores_stage(bi, dil, runs_of(dil, g + 1), 1)
            values_stage(bi, dil, runs_of(dil, g), 0)
            scores_stage(bi, dil, runs_of(dil, g + 2), 0)
            values_stage(bi, dil, runs_of(dil, g + 1), 1)
            return carry

        lax.fori_loop(0, (ngroups - 2) // 2, steady, 0)
        scores_stage(bi, dil, runs_of(dil, ngroups - 1), 1)
        values_stage(bi, dil, runs_of(dil, ngroups - 2), 0)
        if bi + 1 < npat:
            next_dil = patterns[bi + 1][1]
            scores_stage(bi + 1, next_dil, runs_of(next_dil, 0), 0)
        values_stage(bi, dil, runs_of(dil, ngroups - 1), 1)


def _attention(qkv, batch, seq):
    width = qkv.shape[-1] // 3
    pairs = width // LANES
    blk = ATTN_BLOCK
    return pl.pallas_call(
        functools.partial(_attn_kernel, seq=seq),
        grid=(batch, pairs),
        in_specs=[
            pl.BlockSpec((1, seq, LANES), lambda b, h: (b, 0, h)),
            pl.BlockSpec((1, seq, LANES), lambda b, h: (b, 0, pairs + h)),
            pl.BlockSpec((1, seq, LANES), lambda b, h: (b, 0, 2 * pairs + h)),
        ],
        out_specs=pl.BlockSpec((1, seq, LANES), lambda b, h: (b, 0, h)),
        out_shape=jax.ShapeDtypeStruct((batch, seq, width), F32),
        scratch_shapes=[
            pltpu.VMEM((seq, LANES), F32),
            pltpu.VMEM((seq, LANES), F32),
            pltpu.VMEM((seq, LANES), F32),
            pltpu.VMEM((2 * len(DILATED_PATTERNS), 2 * blk, 2 * blk), BF16),
        ] + 2 * [pltpu.VMEM((ATTN_GROUP, 2 * blk, 2 * blk), BF16),
                 pltpu.VMEM((ATTN_GROUP, blk, LANES), F32)],
        compiler_params=pltpu.CompilerParams(
            dimension_semantics=("arbitrary", "arbitrary"), vmem_limit_bytes=VMEM_LIMIT_BYTES),
        name="dilated_attention",
    )(qkv, qkv, qkv)


def _hgrn_tables(chunk):
    levels = int(np.log2(chunk))
    t = np.arange(chunk)
    masks = np.zeros((levels + 1, chunk, chunk), np.float32)
    for l in range(levels):
        blk = t // (1 << l)
        masks[l] = ((blk[:, None] & 1) == 1) & (blk[None, :] == blk[:, None] - 1)
    masks[levels] = np.eye(chunk, dtype=np.float32)
    assert np.array_equal(masks.sum(0), np.tril(np.ones((chunk, chunk), np.float32)))
    return np.tril(np.ones((chunk, chunk), np.float32)), masks


def _hgrn_kernel(q_ref, f_ref, i_ref, gate_ref, lbl_ref, gain_ref, tri_ref, masks_ref,
                 o_ref, state_sc, *slot_refs, tile, chunk, layer, heads):
    hd = HGRN_HEAD_DIM
    levels = masks_ref.shape[0] - 1
    slots = (slot_refs[:4], slot_refs[4:])
    nchunks = tile // chunk
    assert nchunks % 2 == 0 and nchunks >= 4

    @pl.when(pl.program_id(1) == 0)
    def _():
        state_sc[...] = jnp.zeros(state_sc.shape, F32)

    logits = lbl_ref[...]
    e = jnp.exp(logits - jnp.max(logits, axis=0, keepdims=True))
    lb_all = jnp.sum(e[0:layer + 1, :], axis=0, keepdims=True) / jnp.sum(e, axis=0, keepdims=True)
    gain_all = gain_ref[...]
    row = lax.broadcasted_iota(jnp.int32, (chunk, hd), 0)
    odd_row = (row & 1) == 1
    low_half = (row & 4) == 0
    tri = tri_ref[...]

    def chunk_rows(c):
        return pl.ds(pl.multiple_of(c * chunk, chunk), chunk)

    role_sign = [jnp.where(((row >> l) & 1) == 1, 1.0, -1.0) for l in range(levels)]

    def boundary(b_sc, h, m):
        def bcast(r):
            return jnp.broadcast_to(b_sc[h, r:r + 1, :], (SUBLANES, hd))
        pieces = []
        for s in range(chunk // SUBLANES):
            if 2 * m >= SUBLANES:
                pieces.append(bcast((s * SUBLANES) // (2 * m) * (2 * m) + m - 1))
            else:
                pieces.append(jnp.where(low_half[:SUBLANES], bcast(s * SUBLANES + 1),
                                        bcast(s * SUBLANES + 5)))
        return jnp.concatenate(pieces, axis=0)

    def prepare(c, slot):
        qk_sc, aux_sc, b_sc, _ = slots[slot]
        rows = chunk_rows(c)
        for h in range(heads):
            sl = slice(h * hd, (h + 1) * hd)
            lb = lb_all[:, sl]
            qr = q_ref[0, rows, sl]
            sig = jax.nn.sigmoid(f_ref[0, rows, sl])
            f = lb + (1.0 - lb) * sig
            qk_sc[h] = jnp.concatenate([qr * jax.nn.sigmoid(qr), (1.0 - lb) * (1.0 - sig)],
                                       axis=0).astype(BF16)
            aux_sc[h] = jnp.concatenate([jnp.where(odd_row, f, 1.0), i_ref[0, rows, sl]],
                                        axis=0).astype(BF16)
            g = jnp.log2(f)
            g_hi = g.astype(BF16)
            r1 = g - g_hi.astype(F32)
            g_mid = r1.astype(BF16)
            g_lo = (r1 - g_mid.astype(F32)).astype(BF16)
            b_sc[h] = _dot(tri, g_hi) + _dot(tri, g_mid) + _dot(tri, g_lo)

    def mix(slot):
        qk_sc, aux_sc, b_sc, out_sc = slots[slot]
        o_inter = []
        for h in range(heads):
            b = b_sc[h]
            state_t = state_sc[h]
            q_dec = qk_sc[h, :chunk, :] * jnp.exp2(b).astype(BF16)
            o_inter.append(_dot_nt(q_dec, state_t.astype(BF16)))
            b_last = b[chunk - 1:chunk, :]
            k_end = qk_sc[h, chunk:, :] * jnp.exp2(b_last - b).astype(BF16)
            state_sc[h] = state_t * jnp.exp2(b_last) + _dot_tn(aux_sc[h, chunk:, :], k_end)
        a = []
        for h in range(heads):
            qk = qk_sc[h]
            ah = masks_ref[levels] * _dot_nt(qk[:chunk], qk[chunk:])
            for l in range(levels):
                if l == 0:
                    dec = aux_sc[h, :chunk, :]
                else:
                    expo = (b_sc[h] - boundary(b_sc, h, 1 << l)) * role_sign[l]
                    dec = jnp.exp2(expo).astype(BF16)
                prod = qk * jnp.concatenate([dec, dec], axis=0)
                ah = ah + masks_ref[l] * _dot_nt(prod[:chunk], prod[chunk:])
            a.append(ah)
        for h in range(heads):
            out_sc[h] = _dot(a[h].astype(BF16), aux_sc[h, chunk:, :]) + o_inter[h]

    def emit(c, slot):
        out_sc = slots[slot][3]
        rows = chunk_rows(c)
        for h in range(heads):
            sl = slice(h * hd, (h + 1) * hd)
            gt = gate_ref[0, rows, sl]
            o_ref[0, rows, sl] = _rms(out_sc[h], gain_all[:, sl]) * (gt * jax.nn.sigmoid(gt))

    prepare(0, 0)
    prepare(1, 1)
    mix(0)

    def steady(j, carry):
        c = 2 * j + 1
        emit(c - 1, 0)
        prepare(c + 1, 0)
        mix(1)
        emit(c, 1)
        prepare(c + 2, 1)
        mix(0)
        return carry

    lax.fori_loop(0, (nchunks - 2) // 2, steady, 0)
    emit(nchunks - 2, 0)
    mix(1)
    emit(nchunks - 1, 1)


def _hgrn(hg, lb_logits, out_gain, batch, seq, layer):
    width = hg.shape[-1] // 4
    heads = width // HGRN_HEAD_DIM
    chunk = HGRN_CHUNK
    tile = HGRN_SEQ_TILE
    tri, masks = _hgrn_tables(chunk)
    nrows = lb_logits.shape[0]

    def col(k):
        return pl.BlockSpec((1, tile, width), lambda b, t, k=k: (b, t, k))

    return pl.pallas_call(
        functools.partial(_hgrn_kernel, tile=tile, chunk=chunk, layer=layer, heads=heads),
        grid=(batch, seq // tile),
        in_specs=[
            col(0), col(1), col(2), col(3),
            pl.BlockSpec((nrows, width), lambda b, t: (0, 0)),
            pl.BlockSpec((1, width), lambda b, t: (0, 0)),
            pl.BlockSpec(tri.shape, lambda b, t: (0, 0)),
            pl.BlockSpec(masks.shape, lambda b, t: (0, 0, 0)),
        ],
        out_specs=pl.BlockSpec((1, tile, width), lambda b, t: (b, t, 0)),
        out_shape=jax.ShapeDtypeStruct((batch, seq, width), F32),
        scratch_shapes=(
            [pltpu.VMEM((heads, HGRN_HEAD_DIM, HGRN_HEAD_DIM), F32)]
            + 2 * [pltpu.VMEM((heads, 2 * chunk, HGRN_HEAD_DIM), BF16),
                   pltpu.VMEM((heads, 2 * chunk, HGRN_HEAD_DIM), BF16),
                   pltpu.VMEM((heads, chunk, HGRN_HEAD_DIM), F32),
                   pltpu.VMEM((heads, chunk, HGRN_HEAD_DIM), F32)]),
        compiler_params=pltpu.CompilerParams(
            dimension_semantics=("arbitrary", "arbitrary"), vmem_limit_bytes=VMEM_LIMIT_BYTES),
        name="hgrn2",
    )(hg, hg, hg, hg, lb_logits, out_gain, jnp.asarray(tri, BF16), jnp.asarray(masks, F32))


def _out_ffn_kernel(x_ref, a_ref, r_ref, ga_ref, wo_ref, gpost_ref, gpre_ref, w1_ref, w2_ref,
                    gpost2_ref, o_ref, *, ff_tile):
    aw = a_ref.shape[1]
    d_ff = w1_ref.shape[1]
    part = x_ref.shape[0] // FFN_ROW_PARTS
    halves = [pl.ds(i * part, part) for i in range(FFN_ROW_PARTS)]
    x1 = []
    for r in halves:
        an = _rms(a_ref[r, :], ga_ref[...]).astype(BF16)
        mixed = _dot(an, wo_ref[:aw, :]) + _dot(r_ref[r, :].astype(BF16), wo_ref[aw:, :])
        x1.append(x_ref[r, :] + _rms(mixed, gpost_ref[...]))
    for r, x1h in zip(halves, x1):
        h = _rms(x1h, gpre_ref[...]).astype(BF16)
        acc = jnp.zeros(x1h.shape, F32)
        for j in range(d_ff // ff_tile):
            u = _dot(h, w1_ref[:, j * ff_tile:(j + 1) * ff_tile])
            u = jnp.square(jnp.maximum(u, 0.0)).astype(BF16)
            acc = acc + _dot(u, w2_ref[j * ff_tile:(j + 1) * ff_tile, :])
        o_ref[r, :] = x1h + _rms(acc, gpost2_ref[...])


def _out_ffn(x2, attn, rec, ga, wo_b, gpost, gpre, w1_b, w2_b, gpost2):
    n, d = x2.shape
    aw = attn.shape[1]
    rw = rec.shape[1]
    d_ff = w1_b.shape[1]

    def rows(w):
        return pl.BlockSpec((FFN_ROW_TILE, w), lambda i: (i, 0))

    def whole(shape):
        return pl.BlockSpec(shape, lambda i: (0, 0), pipeline_mode=pl.Buffered(1))

    return pl.pallas_call(
        functools.partial(_out_ffn_kernel, ff_tile=FFN_HIDDEN_TILE),
        grid=(n // FFN_ROW_TILE,),
        in_specs=[
            rows(d), rows(aw), rows(rw), whole((1, aw)), whole((aw + rw, d)), whole((1, d)),
            whole((1, d)), whole((d, d_ff)), whole((d_ff, d)), whole((1, d)),
        ],
        out_specs=rows(d),
        out_shape=jax.ShapeDtypeStruct((n, d), F32),
        compiler_params=pltpu.CompilerParams(
            dimension_semantics=("arbitrary",), vmem_limit_bytes=VMEM_LIMIT_BYTES),
        name="outproj_ffn",
    )(x2, attn, rec, ga, wo_b, gpost, gpre, w1_b, w2_b, gpost2)


def kernel(x, mix_pre_norm, w_in, attn_out_norm, hgrn_lb_logits, hgrn_out_norm, w_out,
           mix_post_norm, mlp_pre_norm, w_ff1, w_ff2, mlp_post_norm):
    batch, seq, d = x.shape
    depth = w_in.shape[0]
    aw = attn_out_norm.shape[1]
    n = batch * seq
    x2 = x.reshape(n, d)
    for layer in range(depth):
        qkv, hg = _inproj(x2, mix_pre_norm[layer][None], w_in[layer].astype(BF16), 3 * aw)
        attn = _attention(qkv.reshape(batch, seq, -1), batch, seq)
        rec = _hgrn(hg.reshape(batch, seq, -1), hgrn_lb_logits, hgrn_out_norm[layer][None],
                    batch, seq, layer)
        x2 = _out_ffn(x2, attn.reshape(n, -1), rec.reshape(n, -1), attn_out_norm[layer][None],
                      w_out[layer].astype(BF16), mix_post_norm[layer][None],
                      mlp_pre_norm[layer][None], w_ff1[layer].astype(BF16),
                      w_ff2[layer].astype(BF16), mlp_post_norm[layer][None])
    return x2.reshape(batch, seq, d)
```

```python
import functools

import numpy as np
import jax
import jax.numpy as jnp
from jax import lax
from jax.experimental import pallas as pl
from jax.experimental.pallas import tpu as pltpu

F32 = jnp.float32
BF16 = jnp.bfloat16

RMS_EPS = 1e-6
ATTN_HEAD_DIM = 64
ATTN_BLOCK = 128
ATTN_GROUP = 8
DILATED_PATTERNS = ((128, 1), (512, 4), (2048, 16))
HGRN_HEAD_DIM = 128
HGRN_CHUNK = 64
HGRN_SEQ_TILE = 1024
LANES = 128
SUBLANES = 8
MASKED_SCORE = -1e30
LOG2E = 1.4426950408889634
VMEM_LIMIT_BYTES = 56 * 1024 * 1024
INPROJ_ROW_TILE = 1024
INPROJ_ROW_PARTS = 4
FFN_ROW_TILE = 1024
FFN_HIDDEN_TILE = 1024
FFN_ROW_PARTS = 4


def _rms(x, gain):
    return x * lax.rsqrt(jnp.mean(x * x, axis=-1, keepdims=True) + RMS_EPS) * gain


def _dot(a, b):
    return jnp.dot(a, b, preferred_element_type=F32)


def _dot_nt(a, b):
    return lax.dot_general(a, b, (((1,), (1,)), ((), ())), preferred_element_type=F32)


def _dot_tn(a, b):
    return lax.dot_general(a, b, (((0,), (0,)), ((), ())), preferred_element_type=F32)


def _inproj_kernel(x_ref, g_ref, w_ref, qkv_ref, hg_ref, *, attn_cols):
    part = x_ref.shape[0] // INPROJ_ROW_PARTS
    parts = [pl.ds(i * part, part) for i in range(INPROJ_ROW_PARTS)]
    normed = [_rms(x_ref[r, :], g_ref[...]).astype(BF16) for r in parts]
    for r, h in zip(parts, normed):
        qkv_ref[r, :] = _dot(h, w_ref[:, :attn_cols])
        hg_ref[r, :] = _dot(h, w_ref[:, attn_cols:])


def _inproj(x2, gain, w_in_b, attn_cols):
    n, d = x2.shape
    width = w_in_b.shape[1]
    return pl.pallas_call(
        functools.partial(_inproj_kernel, attn_cols=attn_cols),
        grid=(n // INPROJ_ROW_TILE,),
        in_specs=[
            pl.BlockSpec((INPROJ_ROW_TILE, d), lambda i: (i, 0)),
            pl.BlockSpec((1, d), lambda i: (0, 0)),
            pl.BlockSpec((d, width), lambda i: (0, 0), pipeline_mode=pl.Buffered(1)),
        ],
        out_specs=[
            pl.BlockSpec((INPROJ_ROW_TILE, attn_cols), lambda i: (i, 0)),
            pl.BlockSpec((INPROJ_ROW_TILE, width - attn_cols), lambda i: (i, 0)),
        ],
        out_shape=[
            jax.ShapeDtypeStruct((n, attn_cols), F32),
            jax.ShapeDtypeStruct((n, width - attn_cols), F32),
        ],
        compiler_params=pltpu.CompilerParams(
            dimension_semantics=("arbitrary",), vmem_limit_bytes=VMEM_LIMIT_BYTES),
        name="inproj",
    )(x2, gain, w_in_b)


def _attn_kernel(q_ref, k_ref, v_ref, o_ref, m_sc, acc_sc, den_sc, bias_sc,
                 p0_sc, mb0_sc, p1_sc, mb1_sc, *, seq):
    hp = pl.program_id(1)
    blk = ATTN_BLOCK
    patterns = sorted(DILATED_PATTERNS, key=lambda p: -p[1])
    npat = len(patterns)
    lane =lax.broadcasted_iota(jnp.int32, (blk, LANES), 1)
    head0 = lane < ATTN_HEAD_DIM
    ones_tile = jnp.ones((blk, LANES), BF16)

    row = lax.broadcasted_iota(jnp.int32, (2 * blk, 2 * blk), 0)
    col = lax.broadcasted_iota(jnp.int32, (2 * blk, 2 * blk), 1)
    qi = row & (blk - 1)
    kj = col & (blk - 1)
    is_cur = col >= blk
    dist = jnp.where(is_cur, qi - kj, qi - kj + blk)
    valid = (dist >= 0) & (dist <= blk)
    slope = jnp.exp2((-(2 * hp + (row >> 7) + 1)).astype(F32))
    for bi, (_, dil) in enumerate(patterns):
        bias = -(slope * LOG2E) * (dist * dil).astype(F32)
        bias_sc[2 * bi] = jnp.where(valid, bias, MASKED_SCORE).astype(BF16)
        bias_sc[2 * bi + 1] = jnp.where(valid & is_cur, bias, MASKED_SCORE).astype(BF16)
    slots = ((p0_sc, mb0_sc), (p1_sc, mb1_sc))

    def tile_rows_of(dil, run):
        start, first, count = run
        step = blk * dil

        def rows(st):
            return pl.ds(st, blk, stride=dil) if dil > 1 else pl.ds(st, blk)

        if first is True:
            prev = None
        elif first is False:
            prev = rows(start - step)
        else:
            prev = rows(jnp.where(first, start, start - step))
        return [prev] + [rows(start + g * step) for g in range(count)]

    def scores_stage(bi, dil, runs, slot):
        p_sc, mb_sc = slots[slot]
        blocks = []
        for run in runs:
            tile_rows = tile_rows_of(dil, run)
            first = run[1]
            keys = [None if r is None else k_ref[0, r, :].astype(BF16) for r in tile_rows]
            for g in range(run[2]):
                dynamic_first = g == 0 and not isinstance(first, bool)
                masked_prev = first.astype(jnp.int32) if dynamic_first else 0
                blocks.append((tile_rows[g + 1], keys[g], keys[g + 1], 2 * bi + masked_prev))
        scores = []
        for r, kp, kc, table in blocks:
            qb = q_ref[0, r, :] * (ATTN_HEAD_DIM ** -0.5 * LOG2E)
            qs = jnp.concatenate([jnp.where(head0, qb, 0.0), jnp.where(head0, 0.0, qb)],
                                 axis=0).astype(BF16)
            if kp is None:
                scores.append(_dot_nt(qs, kc).astype(BF16) + bias_sc[table, :, blk:])
            else:
                kcat = jnp.concatenate([kp, kc], axis=0)
                scores.append(_dot_nt(qs, kcat).astype(BF16) + bias_sc[table])
        maxes = [jnp.max(s, axis=-1, keepdims=True) for s in scores]
        for idx, (s, mx) in enumerate(zip(scores, maxes)):
            p_sc[idx, :, :s.shape[1]] = jnp.exp2(s - mx)
            mx = mx.astype(F32)
            mb_sc[idx] = jnp.where(head0, jnp.broadcast_to(mx[:blk], (blk, LANES)),
                                   jnp.broadcast_to(mx[blk:], (blk, LANES)))

    def values_stage(bi, dil, runs, slot):
        p_sc, mb_sc = slots[slot]
        blocks = []
        for run in runs:
            tile_rows = tile_rows_of(dil, run)
            vals = [None if r is None else
                    jnp.concatenate([v_ref[0, r, :].astype(BF16), ones_tile], axis=1)
                    for r in tile_rows]
            for g in range(run[2]):
                blocks.append((tile_rows[g + 1], vals[g], vals[g + 1]))
        outs = []
        for idx, (_, vp, vc) in enumerate(blocks):
            if vp is None:
                outs.append(_dot(p_sc[idx, :, :blk], vc))
            else:
                outs.append(_dot(p_sc[idx], jnp.concatenate([vp, vc], axis=0)))
        for idx, ((r, _, _), out) in enumerate(zip(blocks, outs)):
            acc = jnp.where(head0, out[:blk, :LANES], out[blk:, :LANES])
            den = jnp.where(head0, out[:blk, LANES:], out[blk:, LANES:])
            mb = mb_sc[idx]
            if bi > 0:
                m_old = m_sc[r, :]
                m_new = jnp.maximum(m_old, mb)
                w_old = jnp.exp2(m_old - m_new)
                w_new = jnp.exp2(mb - m_new)
                acc = acc_sc[r, :] * w_old + acc * w_new
                den = den_sc[r, :] * w_old + den * w_new
                mb = m_new
            if bi < npat - 1:
                m_sc[r, :] = mb
                acc_sc[r, :] = acc
                den_sc[r, :] = den
            else:
                o_ref[0, r, :] = acc / den

    ngroups = seq // blk // ATTN_GROUP
    assert ngroups % 2 == 0 and ngroups >= 4

    def runs_of(dil, g):
        run_len = seq // dil // blk
        if run_len >= ATTN_GROUP:
            per_residue = run_len // ATTN_GROUP
            i = g % per_residue
            start = g // per_residue + i * (ATTN_GROUP * blk * dil)
            return [(start, jnp.asarray(i == 0), ATTN_GROUP)]
        nruns = ATTN_GROUP // run_len
        return [(g * nruns + u, True, run_len) for u in range(nruns)]

    scores_stage(0, patterns[0][1], runs_of(patterns[0][1], 0), 0)
    for bi, (_, dil) in enumerate(patterns):

        def steady(j, carry, bi=bi, dil=dil):
            g = 2 * j
            scores_stage(bi, dil, runs_of(dil, g + 1), 1)
            values_stage(bi, dil, runs_of(dil, g), 0)
            scores_stage(bi, dil, runs_of(dil, g + 2), 0)
            values_stage(bi, dil, runs_of(dil, g + 1), 1)
            return carry

        lax.fori_loop(0, (ngroups - 2) // 2, steady, 0)
        scores_stage(bi, dil, runs_of(dil, ngroups - 1), 1)
        values_stage(bi, dil, runs_of(dil, ngroups - 2), 0)
        if bi + 1 < npat:
            next_dil = patterns[bi + 1][1]
            scores_stage(bi + 1, next_dil, runs_of(next_dil, 0), 0)
        values_stage(bi, dil, runs_of(dil, ngroups - 1), 1)


def _attention(qkv, batch, seq):
    width = qkv.shape[-1] // 3
    pairs = width // LANES
    blk = ATTN_BLOCK
    return pl.pallas_call(
        functools.partial(_attn_kernel, seq=seq),
        grid=(batch, pairs),
        in_specs=[
            pl.BlockSpec((1, seq, LANES), lambda b, h: (b, 0, h)),
            pl.BlockSpec((1, seq, LANES), lambda b, h: (b, 0, pairs + h)),
            pl.BlockSpec((1, seq, LANES), lambda b, h: (b, 0, 2 * pairs + h)),
        ],
        out_specs=pl.BlockSpec((1, seq, LANES), lambda b, h: (b, 0, h)),
        out_shape=jax.ShapeDtypeStruct((batch, seq, width), F32),
        scratch_shapes=[
            pltpu.VMEM((seq, LANES), F32),
            pltpu.VMEM((seq, LANES), F32),
            pltpu.VMEM((seq, LANES), F32),
            pltpu.VMEM((2 * len(DILATED_PATTERNS), 2 * blk, 2 * blk), BF16),
        ] + 2 * [pltpu.VMEM((ATTN_GROUP, 2 * blk, 2 * blk), BF16),
                 pltpu.VMEM((ATTN_GROUP, blk, LANES), F32)],
        compiler_params=pltpu.CompilerParams(
            dimension_semantics=("arbitrary", "arbitrary"), vmem_limit_bytes=VMEM_LIMIT_BYTES),
        name="dilated_attention",
    )(qkv, qkv, qkv)


def _hgrn_tables(chunk):
    levels = int(np.log2(chunk))
    t = np.arange(chunk)
    masks = np.zeros((levels + 1, chunk, chunk), np.float32)
    for l in range(levels):
        blk = t // (1 << l)
        masks[l] = ((blk[:, None] & 1) == 1) & (blk[None, :] == blk[:, None] - 1)
    masks[levels] = np.eye(chunk, dtype=np.float32)
    assert np.array_equal(masks.sum(0), np.tril(np.ones((chunk, chunk), np.float32)))
    return np.tril(np.ones((chunk, chunk), np.float32)), masks


def _hgrn_kernel(q_ref, f_ref, i_ref, gate_ref, lbl_ref, gain_ref, tri_ref, masks_ref,
                 o_ref, state_sc, *slot_refs, tile, chunk, layer, heads):
    hd = HGRN_HEAD_DIM
    levels = masks_ref.shape[0] - 1
    slots = (slot_refs[:4], slot_refs[4:])
    nchunks = tile // chunk
    assert nchunks % 2 == 0 and nchunks >= 4

    @pl.when(pl.program_id(1) == 0)
    def _():
        state_sc[...] = jnp.zeros(state_sc.shape, F32)

    logits = lbl_ref[...]
    e = jnp.exp(logits - jnp.max(logits, axis=0, keepdims=True))
    lb_all = jnp.sum(e[0:layer + 1, :], axis=0, keepdims=True) / jnp.sum(e, axis=0, keepdims=True)
    gain_all = gain_ref[...]
    row = lax.broadcasted_iota(jnp.int32, (chunk, hd), 0)
    odd_row = (row & 1) == 1
    low_half = (row & 4) == 0
    tri = tri_ref[...]

    def chunk_rows(c):
        return pl.ds(pl.multiple_of(c * chunk, chunk), chunk)

    role_sign = [jnp.where(((row >> l) & 1) == 1, 1.0, -1.0) for l in range(levels)]

    def boundary(b_sc, h, m):
        def bcast(r):
            return jnp.broadcast_to(b_sc[h, r:r + 1, :], (SUBLANES, hd))
        pieces = []
        for s in range(chunk // SUBLANES):
            if 2 * m >= SUBLANES:
                pieces.append(bcast((s * SUBLANES) // (2 * m) * (2 * m) + m - 1))
            else:
                pieces.append(jnp.where(low_half[:SUBLANES], bcast(s * SUBLANES + 1),
                                        bcast(s * SUBLANES + 5)))
        return jnp.concatenate(pieces, axis=0)

    def prepare(c, slot):
        qk_sc, aux_sc, b_sc, _ = slots[slot]
        rows = chunk_rows(c)
        for h in range(heads):
            sl = slice(h * hd, (h + 1) * hd)
            lb = lb_all[:, sl]
            qr = q_ref[0, rows, sl]
            sig = jax.nn.sigmoid(f_ref[0, rows, sl])
            f = lb + (1.0 - lb) * sig
            qk_sc[h] = jnp.concatenate([qr * jax.nn.sigmoid(qr), (1.0 - lb) * (1.0 - sig)],
                                       axis=0).astype(BF16)
            aux_sc[h] = jnp.concatenate([jnp.where(odd_row, f, 1.0), i_ref[0, rows, sl]],
                                        axis=0).astype(BF16)
            g = jnp.log2(f)
            g_hi = g.astype(BF16)
            r1 = g - g_hi.astype(F32)
            g_mid = r1.astype(BF16)
            g_lo = (r1 - g_mid.astype(F32)).astype(BF16)
            b_sc[h] = _dot(tri, g_hi) + _dot(tri, g_mid) + _dot(tri, g_lo)

    def mix(slot):
        qk_sc, aux_sc, b_sc, out_sc = slots[slot]
        o_inter = []
        for h in range(heads):
            b = b_sc[h]
            state_t = state_sc[h]
            q_dec = qk_sc[h, :chunk, :] * jnp.exp2(b).astype(BF16)
            o_inter.append(_dot_nt(q_dec, state_t.astype(BF16)))
            b_last = b[chunk - 1:chunk, :]
            k_end = qk_sc[h, chunk:, :] * jnp.exp2(b_last - b).astype(BF16)
            state_sc[h] = state_t * jnp.exp2(b_last) + _dot_tn(aux_sc[h, chunk:, :], k_end)
        a = []
        for h in range(heads):
            qk = qk_sc[h]
            ah = masks_ref[levels] * _dot_nt(qk[:chunk], qk[chunk:])
            for l in range(levels):
                if l == 0:
                    dec = aux_sc[h, :chunk, :]
                else:
                    expo = (b_sc[h] - boundary(b_sc, h, 1 << l)) * role_sign[l]
                    dec = jnp.exp2(expo).astype(BF16)
                prod = qk * jnp.concatenate([dec, dec], axis=0)
                ah = ah + masks_ref[l] * _dot_nt(prod[:chunk], prod[chunk:])
            a.append(ah)
        for h in range(heads):
            out_sc[h] = _dot(a[h].astype(BF16), aux_sc[h, chunk:, :]) + o_inter[h]

    def emit(c, slot):
        out_sc = slots[slot][3]
        rows = chunk_rows(c)
        for h in range(heads):
            sl = slice(h * hd, (h + 1) * hd)
            gt = gate_ref[0, rows, sl]
            o_ref[0, rows, sl] = _rms(out_sc[h], gain_all[:, sl]) * (gt * jax.nn.sigmoid(gt))

    prepare(0, 0)
    prepare(1, 1)
    mix(0)

    def steady(j, carry):
        c = 2 * j + 1
        emit(c - 1, 0)
        prepare(c + 1, 0)
        mix(1)
        emit(c, 1)
        prepare(c + 2, 1)
        mix(0)
        return carry

    lax.fori_loop(0, (nchunks - 2) // 2, steady, 0)
    emit(nchunks - 2, 0)
    mix(1)
    emit(nchunks - 1, 1)


def _hgrn(hg, lb_logits, out_gain, batch, seq, layer):
    width = hg.shape[-1] // 4
    heads = width // HGRN_HEAD_DIM
    chunk = HGRN_CHUNK
    tile = HGRN_SEQ_TILE
    tri, masks = _hgrn_tables(chunk)
    nrows = lb_logits.shape[0]

    def col(k):
        return pl.BlockSpec((1, tile, width), lambda b, t, k=k: (b, t, k))

    return pl.pallas_call(
        functools.partial(_hgrn_kernel, tile=tile, chunk=chunk, layer=layer, heads=heads),
        grid=(batch, seq // tile),
        in_specs=[
            col(0), col(1), col(2), col(3),
            pl.BlockSpec((nrows, width), lambda b, t: (0, 0)),
            pl.BlockSpec((1, width), lambda b, t: (0, 0)),
            pl.BlockSpec(tri.shape, lambda b, t: (0, 0)),
            pl.BlockSpec(masks.shape, lambda b, t: (0, 0, 0)),
        ],
        out_specs=pl.BlockSpec((1, tile, width), lambda b, t: (b, t, 0)),
        out_shape=jax.ShapeDtypeStruct((batch, seq, width), F32),
        scratch_shapes=(
            [pltpu.VMEM((heads, HGRN_HEAD_DIM, HGRN_HEAD_DIM), F32)]
            + 2 * [pltpu.VMEM((heads, 2 * chunk, HGRN_HEAD_DIM), BF16),
                   pltpu.VMEM((heads, 2 * chunk, HGRN_HEAD_DIM), BF16),
                   pltpu.VMEM((heads, chunk, HGRN_HEAD_DIM), F32),
                   pltpu.VMEM((heads, chunk, HGRN_HEAD_DIM), F32)]),
        compiler_params=pltpu.CompilerParams(
            dimension_semantics=("arbitrary", "arbitrary"), vmem_limit_bytes=VMEM_LIMIT_BYTES),
        name="hgrn2",
    )(hg, hg, hg, hg, lb_logits, out_gain, jnp.asarray(tri, BF16), jnp.asarray(masks, F32))


def _out_ffn_kernel(x_ref, a_ref, r_ref, ga_ref, wo_ref, gpost_ref, gpre_ref, w1_ref, w2_ref,
                    gpost2_ref, o_ref, *, ff_tile):
    aw = a_ref.shape[1]
    d_ff = w1_ref.shape[1]
    part = x_ref.shape[0] // FFN_ROW_PARTS
    halves = [pl.ds(i * part, part) for i in range(FFN_ROW_PARTS)]
    x1 = []
    for r in halves:
        an = _rms(a_ref[r, :], ga_ref[...]).astype(BF16)
        mixed = _dot(an, wo_ref[:aw, :]) + _dot(r_ref[r, :].astype(BF16), wo_ref[aw:, :])
        x1.append(x_ref[r, :] + _rms(mixed, gpost_ref[...]))
    for r, x1h in zip(halves, x1):
        h = _rms(x1h, gpre_ref[...]).astype(BF16)
        acc = jnp.zeros(x1h.shape, F32)
        for j in range(d_ff // ff_tile):
            u = _dot(h, w1_ref[:, j * ff_tile:(j + 1) * ff_tile])
            u = jnp.square(jnp.maximum(u, 0.0)).astype(BF16)
            acc = acc + _dot(u, w2_ref[j * ff_tile:(j + 1) * ff_tile, :])
        o_ref[r, :] = x1h + _rms(acc, gpost2_ref[...])


def _out_ffn(x2, attn, rec, ga, wo_b, gpost, gpre, w1_b, w2_b, gpost2):
    n, d = x2.shape
    aw = attn.shape[1]
    rw = rec.shape[1]
    d_ff = w1_b.shape[1]

    def rows(w):
        return pl.BlockSpec((FFN_ROW_TILE, w), lambda i: (i, 0))

    def whole(shape):
        return pl.BlockSpec(shape, lambda i: (0, 0), pipeline_mode=pl.Buffered(1))

    return pl.pallas_call(
        functools.partial(_out_ffn_kernel, ff_tile=FFN_HIDDEN_TILE),
        grid=(n // FFN_ROW_TILE,),
        in_specs=[
            rows(d), rows(aw), rows(rw), whole((1, aw)), whole((aw + rw, d)), whole((1, d)),
            whole((1, d)), whole((d, d_ff)), whole((d_ff, d)), whole((1, d)),
        ],
        out_specs=rows(d),
        out_shape=jax.ShapeDtypeStruct((n, d), F32),
        compiler_params=pltpu.CompilerParams(
            dimension_semantics=("arbitrary",), vmem_limit_bytes=VMEM_LIMIT_BYTES),
        name="outproj_ffn",
    )(x2, attn, rec, ga, wo_b, gpost, gpre, w1_b, w2_b, gpost2)


def kernel(x, mix_pre_norm, w_in, attn_out_norm, hgrn_lb_logits, hgrn_out_norm, w_out,
           mix_post_norm, mlp_pre_norm, w_ff1, w_ff2, mlp_post_norm):
    batch, seq, d = x.shape
    depth = w_in.shape[0]
    aw = attn_out_norm.shape[1]
    n = batch * seq
    x2 = x.reshape(n, d)
    for layer in range(depth):
        qkv, hg = _inproj(x2, mix_pre_norm[layer][None], w_in[layer].astype(BF16), 3 * aw)
        attn = _attention(qkv.reshape(batch, seq, -1), batch, seq)
        rec = _hgrn(hg.reshape(batch, seq, -1), hgrn_lb_logits, hgrn_out_norm[layer][None],
                    batch, seq, layer)
        x2 = _out_ffn(x2, attn.reshape(n, -1), rec.reshape(n, -1), attn_out_norm[layer][None],
                      w_out[layer].astype(BF16), mix_post_norm[layer][None],
                      mlp_pre_norm[layer][None], w_ff1[layer].astype(BF16),
                      w_ff2[layer].astype(BF16), mlp_post_norm[layer][None])
    return x2.reshape(batch, seq, d)
```

```python
import functools

import numpy as np
import jax
import jax.numpy as jnp
from jax import lax
from jax.experimental import pallas as pl
from jax.experimental.pallas import tpu as pltpu

F32 = jnp.float32
BF16 = jnp.bfloat16

RMS_EPS = 1e-6
ATTN_HEAD_DIM = 64
ATTN_BLOCK = 128
ATTN_GROUP = 8
DILATED_PATTERNS = ((128, 1), (512, 4), (2048, 16))
HGRN_HEAD_DIM = 128
HGRN_CHUNK = 64
HGRN_SEQ_TILE = 1024
HGRN_STEADY_UNROLL = 7
LANES = 128
SUBLANES = 8
MASKED_SCORE = -1e30
LOG2E = 1.4426950408889634
VMEM_LIMIT_BYTES = 56 * 1024 * 1024
INPROJ_ROW_TILE = 1024
INPROJ_ROW_PARTS = 4
FFN_ROW_TILE = 1024
FFN_HIDDEN_TILE = 1024
FFN_ROW_PARTS = 4


def _rms(x, gain):
    return x * lax.rsqrt(jnp.mean(x * x, axis=-1, keepdims=True) + RMS_EPS) * gain


def _dot(a, b):
    return jnp.dot(a, b, preferred_element_type=F32)


def _dot_nt(a, b):
    return lax.dot_general(a, b, (((1,), (1,)), ((), ())), preferred_element_type=F32)


def _dot_tn(a, b):
    return lax.dot_general(a, b, (((0,), (0,)), ((), ())), preferred_element_type=F32)


def _inproj_kernel(x_ref, g_ref, w_ref, qkv_ref, hg_ref, *, attn_cols):
    part = x_ref.shape[0] // INPROJ_ROW_PARTS
    parts = [pl.ds(i * part, part) for i in range(INPROJ_ROW_PARTS)]
    normed = [_rms(x_ref[r, :], g_ref[...]).astype(BF16) for r in parts]
    for r, h in zip(parts, normed):
        qkv_ref[r, :] = _dot(h, w_ref[:, :attn_cols])
        hg_ref[r, :] = _dot(h, w_ref[:, attn_cols:])


def _inproj(x2, gain, w_in_b, attn_cols):
    n, d = x2.shape
    width = w_in_b.shape[1]
    return pl.pallas_call(
        functools.partial(_inproj_kernel, attn_cols=attn_cols),
        grid=(n // INPROJ_ROW_TILE,),
        in_specs=[
            pl.BlockSpec((INPROJ_ROW_TILE, d), lambda i: (i, 0)),
            pl.BlockSpec((1, d), lambda i: (0, 0)),
            pl.BlockSpec((d, width), lambda i: (0, 0), pipeline_mode=pl.Buffered(1)),
        ],
        out_specs=[
            pl.BlockSpec((INPROJ_ROW_TILE, attn_cols), lambda i: (i, 0)),
            pl.BlockSpec((INPROJ_ROW_TILE, width - attn_cols), lambda i: (i, 0)),
        ],
        out_shape=[
            jax.ShapeDtypeStruct((n, attn_cols), F32),
            jax.ShapeDtypeStruct((n, width - attn_cols), F32),
        ],
        compiler_params=pltpu.CompilerParams(
            dimension_semantics=("arbitrary",), vmem_limit_bytes=VMEM_LIMIT_BYTES),
        name="inproj",
    )(x2, gain, w_in_b)


def _attn_kernel(q_ref, k_ref, v_ref, o_ref, m_sc, acc_sc, den_sc, bias_sc,
                 p0_sc, mb0_sc, p1_sc, mb1_sc, *, seq):
    hp = pl.program_id(1)
    blk = ATTN_BLOCK
    patterns = sorted(DILATED_PATTERNS, key=lambda p: -p[1])
    npat = len(patterns)
    lane =lax.broadcasted_iota(jnp.int32, (blk, LANES), 1)
    head0 = lane < ATTN_HEAD_DIM
    ones_tile = jnp.ones((blk, LANES), BF16)

    row = lax.broadcasted_iota(jnp.int32, (2 * blk, 2 * blk), 0)
    col = lax.broadcasted_iota(jnp.int32, (2 * blk, 2 * blk), 1)
    qi = row & (blk - 1)
    kj = col & (blk - 1)
    is_cur = col >= blk
    dist = jnp.where(is_cur, qi - kj, qi - kj + blk)
    valid = (dist >= 0) & (dist <= blk)
    slope = jnp.exp2((-(2 * hp + (row >> 7) + 1)).astype(F32))
    for bi, (_, dil) in enumerate(patterns):
        bias = -(slope * LOG2E) * (dist * dil).astype(F32)
        bias_sc[2 * bi] = jnp.where(valid, bias, MASKED_SCORE).astype(BF16)
        bias_sc[2 * bi + 1] = jnp.where(valid & is_cur, bias, MASKED_SCORE).astype(BF16)
    slots = ((p0_sc, mb0_sc), (p1_sc, mb1_sc))

    def tile_rows_of(dil, run):
        start, first, count = run
        step = blk * dil

        def rows(st):
            return pl.ds(st, blk, stride=dil) if dil > 1 else pl.ds(st, blk)

        if first is True:
            prev = None
        elif first is False:
            prev = rows(start - step)
        else:
            prev = rows(jnp.where(first, start, start - step))
        return [prev] + [rows(start + g * step) for g in range(count)]

    def scores_stage(bi, dil, runs, slot):
        p_sc, mb_sc = slots[slot]
        blocks = []
        for run in runs:
            tile_rows = tile_rows_of(dil, run)
            first = run[1]
            keys = [None if r is None else k_ref[0, r, :].astype(BF16) for r in tile_rows]
            for g in range(run[2]):
                dynamic_first = g == 0 and not isinstance(first, bool)
                masked_prev = first.astype(jnp.int32) if dynamic_first else 0
                blocks.append((tile_rows[g + 1], keys[g], keys[g + 1], 2 * bi + masked_prev))
        scores = []
        for r, kp, kc, table in blocks:
            qb = q_ref[0, r, :] * (ATTN_HEAD_DIM ** -0.5 * LOG2E)
            qs = jnp.concatenate([jnp.where(head0, qb, 0.0), jnp.where(head0, 0.0, qb)],
                                 axis=0).astype(BF16)
            if kp is None:
                scores.append(_dot_nt(qs, kc).astype(BF16) + bias_sc[table, :, blk:])
            else:
                kcat = jnp.concatenate([kp, kc], axis=0)
                scores.append(_dot_nt(qs, kcat).astype(BF16) + bias_sc[table])
        maxes = [jnp.max(s, axis=-1, keepdims=True) for s in scores]
        for idx, (s, mx) in enumerate(zip(scores, maxes)):
            p_sc[idx, :, :s.shape[1]] = jnp.exp2(s - mx)
            mx = mx.astype(F32)
            mb_sc[idx] = jnp.where(head0, jnp.broadcast_to(mx[:blk], (blk, LANES)),
                                   jnp.broadcast_to(mx[blk:], (blk, LANES)))

    def values_stage(bi, dil, runs, slot):
        p_sc, mb_sc = slots[slot]
        blocks = []
        for run in runs:
            tile_rows = tile_rows_of(dil, run)
            vals = [None if r is None else
                    jnp.concatenate([v_ref[0, r, :].astype(BF16), ones_tile], axis=1)
                    for r in tile_rows]
            for g in range(run[2]):
                blocks.append((tile_rows[g + 1], vals[g], vals[g + 1]))
        outs = []
        for idx, (_, vp, vc) in enumerate(blocks):
            if vp is None:
                outs.append(_dot(p_sc[idx, :, :blk], vc))
            else:
                outs.append(_dot(p_sc[idx], jnp.concatenate([vp, vc], axis=0)))
        for idx, ((r, _, _), out) in enumerate(zip(blocks, outs)):
            acc = jnp.where(head0, out[:blk, :LANES], out[blk:, :LANES])
            den = jnp.where(head0, out[:blk, LANES:], out[blk:, LANES:])
            mb = mb_sc[idx]
            if bi > 0:
                m_old = m_sc[r, :]
                m_new = jnp.maximum(m_old, mb)
                w_old = jnp.exp2(m_old - m_new)
                w_new = jnp.exp2(mb - m_new)
                acc = acc_sc[r, :] * w_old + acc * w_new
                den = den_sc[r, :] * w_old + den * w_new
                mb = m_new
            if bi < npat - 1:
                m_sc[r, :] = mb
                acc_sc[r, :] = acc
                den_sc[r, :] = den
            else:
                o_ref[0, r, :] = acc / den

    ngroups = seq // blk // ATTN_GROUP
    assert ngroups % 2 == 0 and ngroups >= 2

    def runs_of(dil, g):
        run_len = seq // dil // blk
        if run_len >= ATTN_GROUP:
            per_residue = run_len // ATTN_GROUP
            i = g % per_residue
            start = g // per_residue + i * (ATTN_GROUP * blk * dil)
            return [(start, jnp.asarray(i == 0), ATTN_GROUP)]
        nruns = ATTN_GROUP // run_len
        return [(g * nruns + u, True, run_len) for u in range(nruns)]

    scores_stage(0, patterns[0][1], runs_of(patterns[0][1], 0), 0)
    for bi, (_, dil) in enumerate(patterns):

        def steady(j, carry, bi=bi, dil=dil):
            g = 2 * j
            scores_stage(bi, dil, runs_of(dil, g + 1), 1)
            values_stage(bi, dil, runs_of(dil, g), 0)
            scores_stage(bi, dil, runs_of(dil, g + 2), 0)
            values_stage(bi, dil, runs_of(dil, g + 1), 1)
            return carry

        lax.fori_loop(0, (ngroups - 2) // 2, steady, 0)
        scores_stage(bi, dil, runs_of(dil, ngroups - 1), 1)
        values_stage(bi, dil, runs_of(dil, ngroups - 2), 0)
        if bi + 1 < npat:
            next_dil = patterns[bi + 1][1]
            scores_stage(bi + 1, next_dil, runs_of(next_dil, 0), 0)
        values_stage(bi, dil, runs_of(dil, ngroups - 1), 1)


def _attention(qkv, batch, seq):
    width = qkv.shape[-1] // 3
    pairs = width // LANES
    blk = ATTN_BLOCK
    return pl.pallas_call(
        functools.partial(_attn_kernel, seq=seq),
        grid=(batch, pairs),
        in_specs=[
            pl.BlockSpec((1, seq, LANES), lambda b, h: (b, 0, h)),
            pl.BlockSpec((1, seq, LANES), lambda b, h: (b, 0, pairs + h)),
            pl.BlockSpec((1, seq, LANES), lambda b, h: (b, 0, 2 * pairs + h)),
        ],
        out_specs=pl.BlockSpec((1, seq, LANES), lambda b, h: (b, 0, h)),
        out_shape=jax.ShapeDtypeStruct((batch, seq, width), F32),
        scratch_shapes=[
            pltpu.VMEM((seq, LANES), F32),
            pltpu.VMEM((seq, LANES), F32),
            pltpu.VMEM((seq, LANES), F32),
            pltpu.VMEM((2 * len(DILATED_PATTERNS), 2 * blk, 2 * blk), BF16),
        ] + 2 * [pltpu.VMEM((ATTN_GROUP, 2 * blk, 2 * blk), BF16),
                 pltpu.VMEM((ATTN_GROUP, blk, LANES), F32)],
        compiler_params=pltpu.CompilerParams(
            dimension_semantics=("arbitrary", "arbitrary"), vmem_limit_bytes=VMEM_LIMIT_BYTES),
        name="dilated_attention",
    )(qkv, qkv, qkv)


def _hgrn_tables(chunk):
    levels = int(np.log2(chunk))
    t = np.arange(chunk)
    masks = np.zeros((levels + 1, chunk, chunk), np.float32)
    for l in range(levels):
        blk = t // (1 << l)
        masks[l] = ((blk[:, None] & 1) == 1) & (blk[None, :] == blk[:, None] - 1)
    masks[levels] = np.eye(chunk, dtype=np.float32)
    assert np.array_equal(masks.sum(0), np.tril(np.ones((chunk, chunk), np.float32)))
    return np.tril(np.ones((chunk, chunk), np.float32)), masks


def _hgrn_kernel(q_ref, f_ref, i_ref, gate_ref, lbl_ref, gain_ref, tri_ref, masks_ref,
                 o_ref, state_sc, *slot_refs, tile, chunk, layer, heads):
    hd = HGRN_HEAD_DIM
    levels = masks_ref.shape[0] - 1
    slots = (slot_refs[:4], slot_refs[4:])
    nchunks = tile // chunk
    assert nchunks % 2 == 0 and nchunks >= 4

    @pl.when(pl.program_id(1) == 0)
    def _():
        state_sc[...] = jnp.zeros(state_sc.shape, F32)

    logits = lbl_ref[...]
    e = jnp.exp(logits - jnp.max(logits, axis=0, keepdims=True))
    lb_all = jnp.sum(e[0:layer + 1, :], axis=0, keepdims=True) / jnp.sum(e, axis=0, keepdims=True)
    gain_all = gain_ref[...]
    row = lax.broadcasted_iota(jnp.int32, (chunk, hd), 0)
    odd_row = (row & 1) == 1
    low_half = (row & 4) == 0
    tri = tri_ref[...]

    def chunk_rows(c):
        return pl.ds(pl.multiple_of(c * chunk, chunk), chunk)

    role_sign = {l: jnp.where(((row >> l) & 1) == 1, 1.0, -1.0)
                 for l in range(1, levels) if (1 << l) < SUBLANES}

    def level_exponent(b_sc, h, l):
        m = 1 << l

        def bcast(r):
            return jnp.broadcast_to(b_sc[h, r:r + 1, :], (SUBLANES, hd))
        pieces = []
        for s in range(chunk // SUBLANES):
            first = s * SUBLANES
            slab = b_sc[h, first:first + SUBLANES, :]
            if m >= SUBLANES:
                bd = bcast(first // (2 * m) * (2 * m) + m - 1)
                pieces.append(slab - bd if (first // m) % 2 == 1 else bd - slab)
            else:
                if m == 4:
                    bd = bcast(first + 3)
                else:
                    bd = jnp.where(low_half[:SUBLANES], bcast(first + 1), bcast(first + 5))
                pieces.append((slab - bd) * role_sign[l][:SUBLANES])
        return jnp.concatenate(pieces, axis=0)

    def prepare(c, slot):
        qk_sc, aux_sc, b_sc, _ = slots[slot]
        rows = chunk_rows(c)
        for h in range(heads):
            sl = slice(h * hd, (h + 1) * hd)
            lb = lb_all[:, sl]
            qr = q_ref[0, rows, sl]
            sig = jax.nn.sigmoid(f_ref[0, rows, sl])
            f = lb + (1.0 - lb) * sig
            qk_sc[h] = jnp.concatenate([qr * jax.nn.sigmoid(qr), (1.0 - lb) * (1.0 - sig)],
                                       axis=0).astype(BF16)
            aux_sc[h] = jnp.concatenate([jnp.where(odd_row, f, 1.0), i_ref[0, rows, sl]],
                                        axis=0).astype(BF16)
            g = jnp.log2(f)
            g_hi = g.astype(BF16)
            g_lo = (g - g_hi.astype(F32)).astype(BF16)
            b_sc[h] = _dot(tri, g_hi) + _dot(tri, g_lo)

    def mix(slot):
        qk_sc, aux_sc, b_sc, out_sc = slots[slot]
        o_inter = []
        for h in range(heads):
            b = b_sc[h]
            state_t = state_sc[h]
            q_dec = qk_sc[h, :chunk, :] * jnp.exp2(b).astype(BF16)
            o_inter.append(_dot_nt(q_dec, state_t.astype(BF16)))
            b_last = b[chunk - 1:chunk, :]
            k_end = qk_sc[h, chunk:, :] * jnp.exp2(b_last - b).astype(BF16)
            state_sc[h] = state_t * jnp.exp2(b_last) + _dot_tn(aux_sc[h, chunk:, :], k_end)
        a = []
        for h in range(heads):
            qk = qk_sc[h]
            ah = masks_ref[levels] * _dot_nt(qk[:chunk], qk[chunk:])
            for l in range(levels):
                if l == 0:
                    dec = aux_sc[h, :chunk, :]
                else:
                    dec = jnp.exp2(level_exponent(b_sc, h, l)).astype(BF16)
                prod = qk * jnp.concatenate([dec, dec], axis=0)
                ah = ah + masks_ref[l] * _dot_nt(prod[:chunk], prod[chunk:])
            a.append(ah)
        for h in range(heads):
            out_sc[h] = _dot(a[h].astype(BF16), aux_sc[h, chunk:, :]) + o_inter[h]

    def emit(c, slot):
        out_sc = slots[slot][3]
        rows = chunk_rows(c)
        for h in range(heads):
            sl = slice(h * hd, (h + 1) * hd)
            gt = gate_ref[0, rows, sl]
            o_ref[0, rows, sl] = _rms(out_sc[h], gain_all[:, sl]) * (gt * jax.nn.sigmoid(gt))

    prepare(0, 0)
    prepare(1, 1)
    mix(0)

    def steady(j, carry):
        c = 2 * j + 1
        emit(c - 1, 0)
        prepare(c + 1, 0)
        mix(1)
        emit(c, 1)
        prepare(c + 2, 1)
        mix(0)
        return carry

    lax.fori_loop(0, (nchunks - 2) // 2, steady, 0, unroll=HGRN_STEADY_UNROLL)
    emit(nchunks - 2, 0)
    mix(1)
    emit(nchunks - 1, 1)


def _hgrn(hg, lb_logits, out_gain, batch, seq, layer):
    width = hg.shape[-1] // 4
    heads = width // HGRN_HEAD_DIM
    chunk = HGRN_CHUNK
    tile = HGRN_SEQ_TILE
    tri, masks = _hgrn_tables(chunk)
    nrows = lb_logits.shape[0]

    def col(k):
        return pl.BlockSpec((1, tile, width), lambda b, t, k=k: (b, t, k))

    return pl.pallas_call(
        functools.partial(_hgrn_kernel, tile=tile, chunk=chunk, layer=layer, heads=heads),
        grid=(batch, seq // tile),
        in_specs=[
            col(0), col(1), col(2), col(3),
            pl.BlockSpec((nrows, width), lambda b, t: (0, 0)),
            pl.BlockSpec((1, width), lambda b, t: (0, 0)),
            pl.BlockSpec(tri.shape, lambda b, t: (0, 0)),
            pl.BlockSpec(masks.shape, lambda b, t: (0, 0, 0)),
        ],
        out_specs=pl.BlockSpec((1, tile, width), lambda b, t: (b, t, 0)),
        out_shape=jax.ShapeDtypeStruct((batch, seq, width), F32),
        scratch_shapes=(
            [pltpu.VMEM((heads, HGRN_HEAD_DIM, HGRN_HEAD_DIM), F32)]
            + 2 * [pltpu.VMEM((heads, 2 * chunk, HGRN_HEAD_DIM), BF16),
                   pltpu.VMEM((heads, 2 * chunk, HGRN_HEAD_DIM), BF16),
                   pltpu.VMEM((heads, chunk, HGRN_HEAD_DIM), F32),
                   pltpu.VMEM((heads, chunk, HGRN_HEAD_DIM), F32)]),
        compiler_params=pltpu.CompilerParams(
            dimension_semantics=("arbitrary", "arbitrary"), vmem_limit_bytes=VMEM_LIMIT_BYTES),
        name="hgrn2",
    )(hg, hg, hg, hg, lb_logits, out_gain, jnp.asarray(tri, BF16), jnp.asarray(masks, F32))


def _out_ffn_kernel(x_ref, a_ref, r_ref, ga_ref, wo_ref, gpost_ref, gpre_ref, w1_ref, w2_ref,
                    gpost2_ref, o_ref, *, ff_tile):
    aw = a_ref.shape[1]
    d_ff = w1_ref.shape[1]
    part = x_ref.shape[0] // FFN_ROW_PARTS
    halves = [pl.ds(i * part, part) for i in range(FFN_ROW_PARTS)]
    x1 = []
    for r in halves:
        an = _rms(a_ref[r, :], ga_ref[...]).astype(BF16)
        mixed = _dot(an, wo_ref[:aw, :]) + _dot(r_ref[r, :].astype(BF16), wo_ref[aw:, :])
        x1.append(x_ref[r, :] + _rms(mixed, gpost_ref[...]))
    for r, x1h in zip(halves, x1):
        h = _rms(x1h, gpre_ref[...]).astype(BF16)
        acc = jnp.zeros(x1h.shape, F32)
        for j in range(d_ff // ff_tile):
            u = _dot(h, w1_ref[:, j * ff_tile:(j + 1) * ff_tile])
            u = jnp.square(jnp.maximum(u, 0.0)).astype(BF16)
            acc = acc + _dot(u, w2_ref[j * ff_tile:(j + 1) * ff_tile, :])
        o_ref[r, :] = x1h + _rms(acc, gpost2_ref[...])


def _out_ffn(x2, attn, rec, ga, wo_b, gpost, gpre, w1_b, w2_b, gpost2):
    n, d = x2.shape
    aw = attn.shape[1]
    rw = rec.shape[1]
    d_ff = w1_b.shape[1]

    def rows(w):
        return pl.BlockSpec((FFN_ROW_TILE, w), lambda i: (i, 0))

    def whole(shape):
        return pl.BlockSpec(shape, lambda i: (0, 0), pipeline_mode=pl.Buffered(1))

    return pl.pallas_call(
        functools.partial(_out_ffn_kernel, ff_tile=FFN_HIDDEN_TILE),
        grid=(n // FFN_ROW_TILE,),
        in_specs=[
            rows(d), rows(aw), rows(rw), whole((1, aw)), whole((aw + rw, d)), whole((1, d)),
            whole((1, d)), whole((d, d_ff)), whole((d_ff, d)), whole((1, d)),
        ],
        out_specs=rows(d),
        out_shape=jax.ShapeDtypeStruct((n, d), F32),
        compiler_params=pltpu.CompilerParams(
            dimension_semantics=("arbitrary",), vmem_limit_bytes=VMEM_LIMIT_BYTES),
        name="outproj_ffn",
    )(x2, attn, rec, ga, wo_b, gpost, gpre, w1_b, w2_b, gpost2)


def kernel(x, mix_pre_norm, w_in, attn_out_norm, hgrn_lb_logits, hgrn_out_norm, w_out,
           mix_post_norm, mlp_pre_norm, w_ff1, w_ff2, mlp_post_norm):
    batch, seq, d = x.shape
    depth = w_in.shape[0]
    aw = attn_out_norm.shape[1]
    n = batch * seq
    x2 = x.reshape(n, d)
    for layer in range(depth):
        qkv, hg = _inproj(x2, mix_pre_norm[layer][None], w_in[layer].astype(BF16), 3 * aw)
        attn = _attention(qkv.reshape(batch, seq, -1), batch, seq)
        rec = _hgrn(hg.reshape(batch, seq, -1), hgrn_lb_logits, hgrn_out_norm[layer][None],
                    batch, seq, layer)
        x2 = _out_ffn(x2, attn.reshape(n, -1), rec.reshape(n, -1), attn_out_norm[layer][None],
                      w_out[layer].astype(BF16), mix_post_norm[layer][None],
                      mlp_pre_norm[layer][None], w_ff1[layer].astype(BF16),
                      w_ff2[layer].astype(BF16), mlp_post_norm[layer][None])
    return x2.reshape(batch, seq, d)
```

```python
import functools

import numpy as np
import jax
import jax.numpy as jnp
from jax import lax
from jax.experimental import pallas as pl
from jax.experimental.pallas import tpu as pltpu

F32 = jnp.float32
BF16 = jnp.bfloat16

RMS_EPS = 1e-6
ATTN_HEAD_DIM = 64
ATTN_BLOCK = 128
ATTN_GROUP = 8
DILATED_PATTERNS = ((128, 1), (512, 4), (2048, 16))
HGRN_HEAD_DIM = 128
HGRN_CHUNK = 128
HGRN_SEQ_TILE = 1024
LANES = 128
SUBLANES = 8
MASKED_SCORE = -1e30
LOG2E = 1.4426950408889634
VMEM_LIMIT_BYTES = 56 * 1024 * 1024
INPROJ_ROW_TILE = 1024
INPROJ_ROW_PARTS = 4
FFN_ROW_TILE = 1024
FFN_HIDDEN_TILE = 1024
FFN_ROW_PARTS = 4


def _rms(x, gain):
    return x * lax.rsqrt(jnp.mean(x * x, axis=-1, keepdims=True) + RMS_EPS) * gain


def _dot(a, b):
    return jnp.dot(a, b, preferred_element_type=F32)


def _dot_nt(a, b):
    return lax.dot_general(a, b, (((1,), (1,)), ((), ())), preferred_element_type=F32)


def _dot_tn(a, b):
    return lax.dot_general(a, b, (((0,), (0,)), ((), ())), preferred_element_type=F32)


def _inproj_kernel(x_ref, g_ref, w_ref, qkv_ref, hg_ref, *, attn_cols):
    part = x_ref.shape[0] // INPROJ_ROW_PARTS
    parts = [pl.ds(i * part, part) for i in range(INPROJ_ROW_PARTS)]
    normed = [_rms(x_ref[r, :], g_ref[...]).astype(BF16) for r in parts]
    for r, h in zip(parts, normed):
        qkv_ref[r, :] = _dot(h, w_ref[:, :attn_cols])
        hg_ref[r, :] = _dot(h, w_ref[:, attn_cols:])


def _inproj(x2, gain, w_in_b, attn_cols):
    n, d = x2.shape
    width = w_in_b.shape[1]
    return pl.pallas_call(
        functools.partial(_inproj_kernel, attn_cols=attn_cols),
        grid=(n // INPROJ_ROW_TILE,),
        in_specs=[
            pl.BlockSpec((INPROJ_ROW_TILE, d), lambda i: (i, 0)),
            pl.BlockSpec((1, d), lambda i: (0, 0)),
            pl.BlockSpec((d, width), lambda i: (0, 0), pipeline_mode=pl.Buffered(1)),
        ],
        out_specs=[
            pl.BlockSpec((INPROJ_ROW_TILE, attn_cols), lambda i: (i, 0)),
            pl.BlockSpec((INPROJ_ROW_TILE, width - attn_cols), lambda i: (i, 0)),
        ],
        out_shape=[
            jax.ShapeDtypeStruct((n, attn_cols), F32),
            jax.ShapeDtypeStruct((n, width - attn_cols), F32),
        ],
        compiler_params=pltpu.CompilerParams(
            dimension_semantics=("arbitrary",), vmem_limit_bytes=VMEM_LIMIT_BYTES),
        name="inproj",
    )(x2, gain, w_in_b)


def _attn_kernel(q_ref, k_ref, v_ref, o_ref, m_sc, acc_sc, den_sc, bias_sc,
                 p0_sc, mb0_sc, p1_sc, mb1_sc, *, seq):
    hp = pl.program_id(1)
    blk = ATTN_BLOCK
    patterns = sorted(DILATED_PATTERNS, key=lambda p: -p[1])
    npat = len(patterns)
    lane =lax.broadcasted_iota(jnp.int32, (blk, LANES), 1)
    head0 = lane < ATTN_HEAD_DIM
    ones_tile = jnp.ones((blk, LANES), BF16)

    row = lax.broadcasted_iota(jnp.int32, (2 * blk, 2 * blk), 0)
    col = lax.broadcasted_iota(jnp.int32, (2 * blk, 2 * blk), 1)
    qi = row & (blk - 1)
    kj = col & (blk - 1)
    is_cur = col >= blk
    dist = jnp.where(is_cur, qi - kj, qi - kj + blk)
    valid = (dist >= 0) & (dist <= blk)
    slope = jnp.exp2((-(2 * hp + (row >> 7) + 1)).astype(F32))
    for bi, (_, dil) in enumerate(patterns):
        bias = -(slope * LOG2E) * (dist * dil).astype(F32)
        bias_sc[2 * bi] = jnp.where(valid, bias, MASKED_SCORE).astype(BF16)
        bias_sc[2 * bi + 1] = jnp.where(valid & is_cur, bias, MASKED_SCORE).astype(BF16)
    slots = ((p0_sc, mb0_sc), (p1_sc, mb1_sc))

    def tile_rows_of(dil, run):
        start, first, count = run
        step = blk * dil

        def rows(st):
            return pl.ds(st, blk, stride=dil) if dil > 1 else pl.ds(st, blk)

        if first is True:
            prev = None
        elif first is False:
            prev = rows(start - step)
        else:
            prev = rows(jnp.where(first, start, start - step))
        return [prev] + [rows(start + g * step) for g in range(count)]

    def scores_stage(bi, dil, runs, slot):
        p_sc, mb_sc = slots[slot]
        blocks = []
        for run in runs:
            tile_rows = tile_rows_of(dil, run)
            first = run[1]
            keys = [None if r is None else k_ref[0, r, :].astype(BF16) for r in tile_rows]
            for g in range(run[2]):
                dynamic_first = g == 0 and not isinstance(first, bool)
                masked_prev = first.astype(jnp.int32) if dynamic_first else 0
                blocks.append((tile_rows[g + 1], keys[g], keys[g + 1], 2 * bi + masked_prev))
        scores = []
        for r, kp, kc, table in blocks:
            qb = q_ref[0, r, :] * (ATTN_HEAD_DIM ** -0.5 * LOG2E)
            qs = jnp.concatenate([jnp.where(head0, qb, 0.0), jnp.where(head0, 0.0, qb)],
                                 axis=0).astype(BF16)
            if kp is None:
                scores.append(_dot_nt(qs, kc).astype(BF16) + bias_sc[table, :, blk:])
            else:
                kcat = jnp.concatenate([kp, kc], axis=0)
                scores.append(_dot_nt(qs, kcat).astype(BF16) + bias_sc[table])
        maxes = [jnp.max(s, axis=-1, keepdims=True) for s in scores]
        for idx, (s, mx) in enumerate(zip(scores, maxes)):
            p_sc[idx, :, :s.shape[1]] = jnp.exp2(s - mx)
            mx = mx.astype(F32)
            mb_sc[idx] = jnp.where(head0, jnp.broadcast_to(mx[:blk], (blk, LANES)),
                                   jnp.broadcast_to(mx[blk:], (blk, LANES)))

    def values_stage(bi, dil, runs, slot):
        p_sc, mb_sc = slots[slot]
        blocks = []
        for run in runs:
            tile_rows = tile_rows_of(dil, run)
            vals = [None if r is None else
                    jnp.concatenate([v_ref[0, r, :].astype(BF16), ones_tile], axis=1)
                    for r in tile_rows]
            for g in range(run[2]):
                blocks.append((tile_rows[g + 1], vals[g], vals[g + 1]))
        outs = []
        for idx, (_, vp, vc) in enumerate(blocks):
            if vp is None:
                outs.append(_dot(p_sc[idx, :, :blk], vc))
            else:
                outs.append(_dot(p_sc[idx], jnp.concatenate([vp, vc], axis=0)))
        for idx, ((r, _, _), out) in enumerate(zip(blocks, outs)):
            acc = jnp.where(head0, out[:blk, :LANES], out[blk:, :LANES])
            den = jnp.where(head0, out[:blk, LANES:], out[blk:, LANES:])
            mb = mb_sc[idx]
            if bi > 0:
                m_old = m_sc[r, :]
                m_new = jnp.maximum(m_old, mb)
                w_old = jnp.exp2(m_old - m_new)
                w_new = jnp.exp2(mb - m_new)
                acc = acc_sc[r, :] * w_old + acc * w_new
                den = den_sc[r, :] * w_old + den * w_new
                mb = m_new
            if bi < npat - 1:
                m_sc[r, :] = mb
                acc_sc[r, :] = acc
                den_sc[r, :] = den
            else:
                o_ref[0, r, :] = acc / den

    ngroups = seq // blk // ATTN_GROUP
    assert ngroups % 2 == 0 and ngroups >= 2

    def runs_of(dil, g):
        run_len = seq // dil // blk
        if run_len >= ATTN_GROUP:
            per_residue = run_len // ATTN_GROUP
            i = g % per_residue
            start = g // per_residue + i * (ATTN_GROUP * blk * dil)
            return [(start, jnp.asarray(i == 0), ATTN_GROUP)]
        nruns = ATTN_GROUP // run_len
        return [(g * nruns + u, True, run_len) for u in range(nruns)]

    scores_stage(0, patterns[0][1], runs_of(patterns[0][1], 0), 0)
    for bi, (_, dil) in enumerate(patterns):

        def steady(j, carry, bi=bi, dil=dil):
            g = 2 * j
            scores_stage(bi, dil, runs_of(dil, g + 1), 1)
            values_stage(bi, dil, runs_of(dil, g), 0)
            scores_stage(bi, dil, runs_of(dil, g + 2), 0)
            values_stage(bi, dil, runs_of(dil, g + 1), 1)
            return carry

        lax.fori_loop(0, (ngroups - 2) // 2, steady, 0)
        scores_stage(bi, dil, runs_of(dil, ngroups - 1), 1)
        values_stage(bi, dil, runs_of(dil, ngroups - 2), 0)
        if bi + 1 < npat:
            next_dil = patterns[bi + 1][1]
            scores_stage(bi + 1, next_dil, runs_of(next_dil, 0), 0)
        values_stage(bi, dil, runs_of(dil, ngroups - 1), 1)


def _attention(qkv, batch, seq):
    width = qkv.shape[-1] // 3
    pairs = width // LANES
    blk = ATTN_BLOCK
    return pl.pallas_call(
        functools.partial(_attn_kernel, seq=seq),
        grid=(batch, pairs),
        in_specs=[
            pl.BlockSpec((1, seq, LANES), lambda b, h: (b, 0, h)),
            pl.BlockSpec((1, seq, LANES), lambda b, h: (b, 0, pairs + h)),
            pl.BlockSpec((1, seq, LANES), lambda b, h: (b, 0, 2 * pairs + h)),
        ],
        out_specs=pl.BlockSpec((1, seq, LANES), lambda b, h: (b, 0, h)),
        out_shape=jax.ShapeDtypeStruct((batch, seq, width), F32),
        scratch_shapes=[
            pltpu.VMEM((seq, LANES), F32),
            pltpu.VMEM((seq, LANES), F32),
            pltpu.VMEM((seq, LANES), F32),
            pltpu.VMEM((2 * len(DILATED_PATTERNS), 2 * blk, 2 * blk), BF16),
        ] + 2 * [pltpu.VMEM((ATTN_GROUP, 2 * blk, 2 * blk), BF16),
                 pltpu.VMEM((ATTN_GROUP, blk, LANES), F32)],
        compiler_params=pltpu.CompilerParams(
            dimension_semantics=("arbitrary", "arbitrary"), vmem_limit_bytes=VMEM_LIMIT_BYTES),
        name="dilated_attention",
    )(qkv, qkv, qkv)


def _hgrn_tables(chunk):
    levels = int(np.log2(chunk))
    t = np.arange(chunk)
    masks = np.zeros((levels + 1, chunk, chunk), np.float32)
    for l in range(levels):
        blk = t // (1 << l)
        masks[l] = ((blk[:, None] & 1) == 1) & (blk[None, :] == blk[:, None] - 1)
    masks[levels] = np.eye(chunk, dtype=np.float32)
    assert np.array_equal(masks.sum(0), np.tril(np.ones((chunk, chunk), np.float32)))
    return np.tril(np.ones((chunk, chunk), np.float32)), masks


def _hgrn_kernel(q_ref, f_ref, i_ref, gate_ref, lbl_ref, gain_ref, tri_ref, masks_ref,
                 o_ref, state_sc, *slot_refs, tile, chunk, layer, heads):
    hd = HGRN_HEAD_DIM
    levels = masks_ref.shape[0] - 1
    slots = (slot_refs[:4], slot_refs[4:])
    nchunks = tile // chunk
    assert nchunks % 2 == 0 and nchunks >= 4

    @pl.when(pl.program_id(1) == 0)
    def _():
        state_sc[...] = jnp.zeros(state_sc.shape, F32)

    logits = lbl_ref[...]
    e = jnp.exp(logits - jnp.max(logits, axis=0, keepdims=True))
    lb_all = jnp.sum(e[0:layer + 1, :], axis=0, keepdims=True) / jnp.sum(e, axis=0, keepdims=True)
    gain_all = gain_ref[...]
    row = lax.broadcasted_iota(jnp.int32, (chunk, hd), 0)
    odd_row = (row & 1) == 1
    low_half = (row & 4) == 0
    tri = tri_ref[...]

    def chunk_rows(c):
        return pl.ds(pl.multiple_of(c * chunk, chunk), chunk)

    role_sign = {l: jnp.where(((row >> l) & 1) == 1, 1.0, -1.0)
                 for l in range(1, levels) if (1 << l) < SUBLANES}

    def level_exponent(b_sc, h, l):
        m = 1 << l

        def bcast(r):
            return jnp.broadcast_to(b_sc[h, r:r + 1, :], (SUBLANES, hd))
        pieces = []
        for s in range(chunk // SUBLANES):
            first = s * SUBLANES
            slab = b_sc[h, first:first + SUBLANES, :]
            if m >= SUBLANES:
                bd = bcast(first // (2 * m) * (2 * m) + m - 1)
                pieces.append(slab - bd if (first // m) % 2 == 1 else bd - slab)
            else:
                if m == 4:
                    bd = bcast(first + 3)
                else:
                    bd = jnp.where(low_half[:SUBLANES], bcast(first + 1), bcast(first + 5))
                pieces.append((slab - bd) * role_sign[l][:SUBLANES])
        return jnp.concatenate(pieces, axis=0)

    def prepare(c, slot):
        qk_sc, aux_sc, b_sc, _ = slots[slot]
        rows = chunk_rows(c)
        for h in range(heads):
            sl = slice(h * hd, (h + 1) * hd)
            lb = lb_all[:, sl]
            qr = q_ref[0, rows, sl]
            sig = jax.nn.sigmoid(f_ref[0, rows, sl])
            f = lb + (1.0 - lb) * sig
            qk_sc[h] = jnp.concatenate([qr * jax.nn.sigmoid(qr), (1.0 - lb) * (1.0 - sig)],
                                       axis=0).astype(BF16)
            aux_sc[h] = jnp.concatenate([jnp.where(odd_row, f, 1.0), i_ref[0, rows, sl]],
                                        axis=0).astype(BF16)
            g = jnp.log2(f)
            g_hi = g.astype(BF16)
            g_lo = (g - g_hi.astype(F32)).astype(BF16)
            b_sc[h] = _dot(tri, g_hi) + _dot(tri, g_lo)

    def mix(slot):
        qk_sc, aux_sc, b_sc, out_sc = slots[slot]
        o_inter = []
        for h in range(heads):
            b = b_sc[h]
            state_t = state_sc[h]
            q_dec = qk_sc[h, :chunk, :] * jnp.exp2(b).astype(BF16)
            o_inter.append(_dot_nt(q_dec, state_t.astype(BF16)))
            b_last = b[chunk - 1:chunk, :]
            k_end = qk_sc[h, chunk:, :] * jnp.exp2(b_last - b).astype(BF16)
            state_sc[h] = state_t * jnp.exp2(b_last) + _dot_tn(aux_sc[h, chunk:, :], k_end)
        a = []
        for h in range(heads):
            qk = qk_sc[h]
            ah = masks_ref[levels] * _dot_nt(qk[:chunk], qk[chunk:])
            for l in range(levels):
                if l == 0:
                    dec = aux_sc[h, :chunk, :]
                else:
                    dec = jnp.exp2(level_exponent(b_sc, h, l)).astype(BF16)
                prod = qk * jnp.concatenate([dec, dec], axis=0)
                ah = ah + masks_ref[l] * _dot_nt(prod[:chunk], prod[chunk:])
            a.append(ah)
        for h in range(heads):
            out_sc[h] = _dot(a[h].astype(BF16), aux_sc[h, chunk:, :]) + o_inter[h]

    def emit(c, slot):
        out_sc = slots[slot][3]
        rows = chunk_rows(c)
        for h in range(heads):
            sl = slice(h * hd, (h + 1) * hd)
            gt = gate_ref[0, rows, sl]
            o_ref[0, rows, sl] = _rms(out_sc[h], gain_all[:, sl]) * (gt * jax.nn.sigmoid(gt))

    prepare(0, 0)
    prepare(1, 1)
    mix(0)

    def steady(j, carry):
        c = 2 * j + 1
        emit(c - 1, 0)
        prepare(c + 1, 0)
        mix(1)
        emit(c, 1)
        prepare(c + 2, 1)
        mix(0)
        return carry

    lax.fori_loop(0, (nchunks - 2) // 2, steady, 0, unroll=True)
    emit(nchunks - 2, 0)
    mix(1)
    emit(nchunks - 1, 1)


def _hgrn(hg, lb_logits, out_gain, batch, seq, layer):
    width = hg.shape[-1] // 4
    heads = width // HGRN_HEAD_DIM
    chunk = HGRN_CHUNK
    tile = HGRN_SEQ_TILE
    tri, masks = _hgrn_tables(chunk)
    nrows = lb_logits.shape[0]

    def col(k):
        return pl.BlockSpec((1, tile, width), lambda b, t, k=k: (b, t, k))

    return pl.pallas_call(
        functools.partial(_hgrn_kernel, tile=tile, chunk=chunk, layer=layer, heads=heads),
        grid=(batch, seq // tile),
        in_specs=[
            col(0), col(1), col(2), col(3),
            pl.BlockSpec((nrows, width), lambda b, t: (0, 0)),
            pl.BlockSpec((1, width), lambda b, t: (0, 0)),
            pl.BlockSpec(tri.shape, lambda b, t: (0, 0)),
            pl.BlockSpec(masks.shape, lambda b, t: (0, 0, 0)),
        ],
        out_specs=pl.BlockSpec((1, tile, width), lambda b, t: (b, t, 0)),
        out_shape=jax.ShapeDtypeStruct((batch, seq, width), F32),
        scratch_shapes=(
            [pltpu.VMEM((heads, HGRN_HEAD_DIM, HGRN_HEAD_DIM), F32)]
            + 2 * [pltpu.VMEM((heads, 2 * chunk, HGRN_HEAD_DIM), BF16),
                   pltpu.VMEM((heads, 2 * chunk, HGRN_HEAD_DIM), BF16),
                   pltpu.VMEM((heads, chunk, HGRN_HEAD_DIM), F32),
                   pltpu.VMEM((heads, chunk, HGRN_HEAD_DIM), F32)]),
        compiler_params=pltpu.CompilerParams(
            dimension_semantics=("arbitrary", "arbitrary"), vmem_limit_bytes=VMEM_LIMIT_BYTES),
        name="hgrn2",
    )(hg, hg, hg, hg, lb_logits, out_gain, jnp.asarray(tri, BF16), jnp.asarray(masks, F32))


def _out_ffn_kernel(x_ref, a_ref, r_ref, ga_ref, wo_ref, gpost_ref, gpre_ref, w1_ref, w2_ref,
                    gpost2_ref, o_ref, *, ff_tile):
    aw = a_ref.shape[1]
    d_ff = w1_ref.shape[1]
    part = x_ref.shape[0] // FFN_ROW_PARTS
    halves = [pl.ds(i * part, part) for i in range(FFN_ROW_PARTS)]
    x1 = []
    for r in halves:
        an = _rms(a_ref[r, :], ga_ref[...]).astype(BF16)
        mixed = _dot(an, wo_ref[:aw, :]) + _dot(r_ref[r, :].astype(BF16), wo_ref[aw:, :])
        x1.append(x_ref[r, :] + _rms(mixed, gpost_ref[...]))
    for r, x1h in zip(halves, x1):
        h = _rms(x1h, gpre_ref[...]).astype(BF16)
        acc = jnp.zeros(x1h.shape, F32)
        for j in range(d_ff // ff_tile):
            u = _dot(h, w1_ref[:, j * ff_tile:(j + 1) * ff_tile])
            u = jnp.square(jnp.maximum(u, 0.0)).astype(BF16)
            acc = acc + _dot(u, w2_ref[j * ff_tile:(j + 1) * ff_tile, :])
        o_ref[r, :] = x1h + _rms(acc, gpost2_ref[...])


def _out_ffn(x2, attn, rec, ga, wo_b, gpost, gpre, w1_b, w2_b, gpost2):
    n, d = x2.shape
    aw = attn.shape[1]
    rw = rec.shape[1]
    d_ff = w1_b.shape[1]

    def rows(w):
        return pl.BlockSpec((FFN_ROW_TILE, w), lambda i: (i, 0))

    def whole(shape):
        return pl.BlockSpec(shape, lambda i: (0, 0), pipeline_mode=pl.Buffered(1))

    return pl.pallas_call(
        functools.partial(_out_ffn_kernel, ff_tile=FFN_HIDDEN_TILE),
        grid=(n // FFN_ROW_TILE,),
        in_specs=[
            rows(d), rows(aw), rows(rw), whole((1, aw)), whole((aw + rw, d)), whole((1, d)),
            whole((1, d)), whole((d, d_ff)), whole((d_ff, d)), whole((1, d)),
        ],
        out_specs=rows(d),
        out_shape=jax.ShapeDtypeStruct((n, d), F32),
        compiler_params=pltpu.CompilerParams(
            dimension_semantics=("arbitrary",), vmem_limit_bytes=VMEM_LIMIT_BYTES),
        name="outproj_ffn",
    )(x2, attn, rec, ga, wo_b, gpost, gpre, w1_b, w2_b, gpost2)


def kernel(x, mix_pre_norm, w_in, attn_out_norm, hgrn_lb_logits, hgrn_out_norm, w_out,
           mix_post_norm, mlp_pre_norm, w_ff1, w_ff2, mlp_post_norm):
    batch, seq, d = x.shape
    depth = w_in.shape[0]
    aw = attn_out_norm.shape[1]
    n = batch * seq
    x2 = x.reshape(n, d)
    for layer in range(depth):
        qkv, hg = _inproj(x2, mix_pre_norm[layer][None], w_in[layer].astype(BF16), 3 * aw)
        attn = _attention(qkv.reshape(batch, seq, -1), batch, seq)
        rec = _hgrn(hg.reshape(batch, seq, -1), hgrn_lb_logits, hgrn_out_norm[layer][None],
                    batch, seq, layer)
        x2 = _out_ffn(x2, attn.reshape(n, -1), rec.reshape(n, -1), attn_out_norm[layer][None],
                      w_out[layer].astype(BF16), mix_post_norm[layer][None],
                      mlp_pre_norm[layer][None], w_ff1[layer].astype(BF16),
                      w_ff2[layer].astype(BF16), mlp_post_norm[layer][None])
    return x2.reshape(batch, seq, d)
```

```python
import functools

import numpy as np
import jax
import jax.numpy as jnp
from jax import lax
from jax.experimental import pallas as pl
from jax.experimental.pallas import tpu as pltpu

F32 = jnp.float32
BF16 = jnp.bfloat16

RMS_EPS = 1e-6
ATTN_HEAD_DIM = 64
ATTN_BLOCK = 128
ATTN_GROUP = 8
DILATED_PATTERNS = ((128, 1), (512, 4), (2048, 16))
HGRN_HEAD_DIM = 128
HGRN_CHUNK = 64
HGRN_SEQ_TILE = 1024
LANES = 128
SUBLANES = 8
MASKED_SCORE = -1e30
LOG2E = 1.4426950408889634
VMEM_LIMIT_BYTES = 56 * 1024 * 1024
INPROJ_ROW_TILE = 1024
INPROJ_ROW_PARTS = 4
FFN_ROW_TILE = 1024
FFN_HIDDEN_TILE = 1024
FFN_ROW_PARTS = 4


def _rms(x, gain):
    return x * lax.rsqrt(jnp.mean(x * x, axis=-1, keepdims=True) + RMS_EPS) * gain


def _dot(a, b):
    return jnp.dot(a, b, preferred_element_type=F32)


def _dot_nt(a, b):
    return lax.dot_general(a, b, (((1,), (1,)), ((), ())), preferred_element_type=F32)


def _dot_tn(a, b):
    return lax.dot_general(a, b, (((0,), (0,)), ((), ())), preferred_element_type=F32)


def _inproj_kernel(x_ref, g_ref, w_ref, qkv_ref, hg_ref, *, attn_cols):
    part = x_ref.shape[0] // INPROJ_ROW_PARTS
    parts = [pl.ds(i * part, part) for i in range(INPROJ_ROW_PARTS)]
    normed = [_rms(x_ref[r, :], g_ref[...]).astype(BF16) for r in parts]
    for r, h in zip(parts, normed):
        qkv_ref[r, :] = _dot(h, w_ref[:, :attn_cols])
        hg_ref[r, :] = _dot(h, w_ref[:, attn_cols:])


def _inproj(x2, gain, w_in_b, attn_cols):
    n, d = x2.shape
    width = w_in_b.shape[1]
    return pl.pallas_call(
        functools.partial(_inproj_kernel, attn_cols=attn_cols),
        grid=(n // INPROJ_ROW_TILE,),
        in_specs=[
            pl.BlockSpec((INPROJ_ROW_TILE, d), lambda i: (i, 0)),
            pl.BlockSpec((1, d), lambda i: (0, 0)),
            pl.BlockSpec((d, width), lambda i: (0, 0), pipeline_mode=pl.Buffered(1)),
        ],
        out_specs=[
            pl.BlockSpec((INPROJ_ROW_TILE, attn_cols), lambda i: (i, 0)),
            pl.BlockSpec((INPROJ_ROW_TILE, width - attn_cols), lambda i: (i, 0)),
        ],
        out_shape=[
            jax.ShapeDtypeStruct((n, attn_cols), F32),
            jax.ShapeDtypeStruct((n, width - attn_cols), F32),
        ],
        compiler_params=pltpu.CompilerParams(
            dimension_semantics=("arbitrary",), vmem_limit_bytes=VMEM_LIMIT_BYTES),
        name="inproj",
    )(x2, gain, w_in_b)


def _attn_kernel(q_ref, k_ref, v_ref, o_ref, m_sc, acc_sc, den_sc, bias_sc, *, seq):
    hp = pl.program_id(1)
    blk = ATTN_BLOCK
    patterns = sorted(DILATED_PATTERNS, key=lambda p: -p[1])
    npat = len(patterns)
    lane = lax.broadcasted_iota(jnp.int32, (blk, LANES), 1)
    head0 = lane < ATTN_HEAD_DIM
    ones_tile = jnp.ones((blk, LANES), BF16)

    row = lax.broadcasted_iota(jnp.int32, (2 * blk, 2 * blk), 0)
    col = lax.broadcasted_iota(jnp.int32, (2 * blk, 2 * blk), 1)
    qi = row & (blk - 1)
    kj = col & (blk - 1)
    is_cur = col >= blk
    dist = jnp.where(is_cur, qi - kj, qi - kj + blk)
    valid = (dist >= 0) & (dist <= blk)
    slope = jnp.exp2((-(2 * hp + (row >> 7) + 1)).astype(F32))
    for bi, (_, dil) in enumerate(patterns):
        bias = -(slope * LOG2E) * (dist * dil).astype(F32)
        bias_sc[bi] = jnp.where(valid, bias, MASKED_SCORE).astype(BF16)

    def tile_rows_of(dil, run):
        start, first, count = run
        step = blk * dil

        def rows(st):
            return pl.ds(st, blk, stride=dil) if dil > 1 else pl.ds(st, blk)

        return [None if first else rows(start - step)] + [
            rows(start + g * step) for g in range(count)]

    def group(bi, dil, runs):
        blocks = []
        for run in runs:
            tile_rows = tile_rows_of(dil, run)
            keys = [None if r is None else k_ref[0, r, :].astype(BF16) for r in tile_rows]
            vals = [None if r is None else
                    jnp.concatenate([v_ref[0, r, :].astype(BF16), ones_tile], axis=1)
                    for r in tile_rows]
            for g in range(run[2]):
                blocks.append((tile_rows[g + 1], keys[g], keys[g + 1], vals[g], vals[g + 1]))
        scores = []
        for r, kp, kc, _, _ in blocks:
            qb = q_ref[0, r, :] * (ATTN_HEAD_DIM ** -0.5 * LOG2E)
            qs = jnp.concatenate([jnp.where(head0, qb, 0.0), jnp.where(head0, 0.0, qb)],
                                 axis=0).astype(BF16)
            if kp is None:
                scores.append(_dot_nt(qs, kc).astype(BF16) + bias_sc[bi, :, blk:])
            else:
                kcat = jnp.concatenate([kp, kc], axis=0)
                scores.append(_dot_nt(qs, kcat).astype(BF16) + bias_sc[bi])
        maxes = [jnp.max(s, axis=-1, keepdims=True) for s in scores]
        probs = [jnp.exp2(s - mx) for s, mx in zip(scores, maxes)]
        outs = []
        for (_, _, _, vp, vc), p in zip(blocks, probs):
            outs.append(_dot(p, vc) if vp is None else _dot(p, jnp.concatenate([vp, vc], axis=0)))
        for (r, _, _, _, _), out, mx in zip(blocks, outs, maxes):
            mx = mx.astype(F32)
            acc = jnp.where(head0, out[:blk, :LANES], out[blk:, :LANES])
            den = jnp.where(head0, out[:blk, LANES:], out[blk:, LANES:])
            mb = jnp.where(head0, jnp.broadcast_to(mx[:blk], (blk, LANES)),
                           jnp.broadcast_to(mx[blk:], (blk, LANES)))
            if bi > 0:
                m_old = m_sc[r, :]
                m_new = jnp.maximum(m_old, mb)
                w_old = jnp.exp2(m_old - m_new)
                w_new = jnp.exp2(mb - m_new)
                acc = acc_sc[r, :] * w_old + acc * w_new
                den = den_sc[r, :] * w_old + den * w_new
                mb = m_new
            if bi < npat - 1:
                m_sc[r, :] = mb
                acc_sc[r, :] = acc
                den_sc[r, :] = den
            else:
                o_ref[0, r, :] = acc / den

    for bi, (_, dil) in enumerate(patterns):
        run_len = seq // dil // blk
        for g in range(seq // blk // ATTN_GROUP):
            if run_len >= ATTN_GROUP:
                per_residue = run_len // ATTN_GROUP
                i = g % per_residue
                runs = [(g // per_residue + i * (ATTN_GROUP * blk * dil), i == 0, ATTN_GROUP)]
            else:
                nruns = ATTN_GROUP // run_len
                runs = [(g * nruns + u, True, run_len) for u in range(nruns)]
            group(bi, dil, runs)


def _attention(qkv, batch, seq):
    width = qkv.shape[-1] // 3
    pairs = width // LANES
    blk = ATTN_BLOCK
    return pl.pallas_call(
        functools.partial(_attn_kernel, seq=seq),
        grid=(batch, pairs),
        in_specs=[
            pl.BlockSpec((1, seq, LANES), lambda b, h: (b, 0, h)),
            pl.BlockSpec((1, seq, LANES), lambda b, h: (b, 0, pairs + h)),
            pl.BlockSpec((1, seq, LANES), lambda b, h: (b, 0, 2 * pairs + h)),
        ],
        out_specs=pl.BlockSpec((1, seq, LANES), lambda b, h: (b, 0, h)),
        out_shape=jax.ShapeDtypeStruct((batch, seq, width), F32),
        scratch_shapes=[
            pltpu.VMEM((seq, LANES), F32),
            pltpu.VMEM((seq, LANES), F32),
            pltpu.VMEM((seq, LANES), F32),
            pltpu.VMEM((len(DILATED_PATTERNS), 2 * blk, 2 * blk), BF16),
        ],
        compiler_params=pltpu.CompilerParams(
            dimension_semantics=("arbitrary", "arbitrary"), vmem_limit_bytes=VMEM_LIMIT_BYTES),
        name="dilated_attention",
    )(qkv, qkv, qkv)


def _hgrn_tables(chunk):
    levels = int(np.log2(chunk))
    t = np.arange(chunk)
    masks = np.zeros((levels + 1, chunk, chunk), np.float32)
    for l in range(levels):
        blk = t // (1 << l)
        masks[l] = ((blk[:, None] & 1) == 1) & (blk[None, :] == blk[:, None] - 1)
    masks[levels] = np.eye(chunk, dtype=np.float32)
    assert np.array_equal(masks.sum(0), np.tril(np.ones((chunk, chunk), np.float32)))
    return np.tril(np.ones((chunk, chunk), np.float32)), masks


def _hgrn_kernel(q_ref, f_ref, i_ref, gate_ref, lbl_ref, gain_ref, tri_ref, masks_ref,
                 o_ref, state_sc, *slot_refs, tile, chunk, layer, heads):
    hd = HGRN_HEAD_DIM
    levels = masks_ref.shape[0] - 1
    slots = (slot_refs[:4], slot_refs[4:])
    nchunks = tile // chunk
    assert nchunks % 2 == 0 and nchunks >= 4

    @pl.when(pl.program_id(1) == 0)
    def _():
        state_sc[...] = jnp.zeros(state_sc.shape, F32)

    logits = lbl_ref[...]
    e = jnp.exp(logits - jnp.max(logits, axis=0, keepdims=True))
    lb_all = jnp.sum(e[0:layer + 1, :], axis=0, keepdims=True) / jnp.sum(e, axis=0, keepdims=True)
    gain_all = gain_ref[...]
    row = lax.broadcasted_iota(jnp.int32, (chunk, hd), 0)
    odd_row = (row & 1) == 1
    low_half = (row & 4) == 0
    tri = tri_ref[...]

    def chunk_rows(c):
        return pl.ds(pl.multiple_of(c * chunk, chunk), chunk)

    role_sign = {l: jnp.where(((row >> l) & 1) == 1, 1.0, -1.0)
                 for l in range(1, levels) if (1 << l) < SUBLANES}

    def level_exponent(b_sc, h, l):
        m = 1 << l

        def bcast(r):
            return jnp.broadcast_to(b_sc[h, r:r + 1, :], (SUBLANES, hd))
        pieces = []
        for s in range(chunk // SUBLANES):
            first = s * SUBLANES
            slab = b_sc[h, first:first + SUBLANES, :]
            if m >= SUBLANES:
                bd = bcast(first // (2 * m) * (2 * m) + m - 1)
                pieces.append(slab - bd if (first // m) % 2 == 1 else bd - slab)
            else:
                if m == 4:
                    bd = bcast(first + 3)
                else:
                    bd = jnp.where(low_half[:SUBLANES], bcast(first + 1), bcast(first + 5))
                pieces.append((slab - bd) * role_sign[l][:SUBLANES])
        return jnp.concatenate(pieces, axis=0)

    def prepare(c, slot):
        qk_sc, aux_sc, b_sc, _ = slots[slot]
        rows = chunk_rows(c)
        for h in range(heads):
            sl = slice(h * hd, (h + 1) * hd)
            lb = lb_all[:, sl]
            qr = q_ref[0, rows, sl]
            sig = jax.nn.sigmoid(f_ref[0, rows, sl])
            f = lb + (1.0 - lb) * sig
            qk_sc[h] = jnp.concatenate([qr * jax.nn.sigmoid(qr), (1.0 - lb) * (1.0 - sig)],
                                       axis=0).astype(BF16)
            aux_sc[h] = jnp.concatenate([jnp.where(odd_row, f, 1.0), i_ref[0, rows, sl]],
                                        axis=0).astype(BF16)
            g = jnp.log2(f)
            g_hi = g.astype(BF16)
            g_lo = (g - g_hi.astype(F32)).astype(BF16)
            b_sc[h] = _dot(tri, g_hi) + _dot(tri, g_lo)

    def mix(slot):
        qk_sc, aux_sc, b_sc, out_sc = slots[slot]
        o_inter = []
        for h in range(heads):
            b = b_sc[h]
            state_t = state_sc[h]
            q_dec = qk_sc[h, :chunk, :] * jnp.exp2(b).astype(BF16)
            o_inter.append(_dot_nt(q_dec, state_t.astype(BF16)))
            b_last = b[chunk - 1:chunk, :]
            k_end = qk_sc[h, chunk:, :] * jnp.exp2(b_last - b).astype(BF16)
            state_sc[h] = state_t * jnp.exp2(b_last) + _dot_tn(aux_sc[h, chunk:, :], k_end)
        a = []
        for h in range(heads):
            qk = qk_sc[h]
            ah = masks_ref[levels] * _dot_nt(qk[:chunk], qk[chunk:])
            for l in range(levels):
                if l == 0:
                    dec = aux_sc[h, :chunk, :]
                else:
                    dec = jnp.exp2(level_exponent(b_sc, h, l)).astype(BF16)
                prod = qk * jnp.concatenate([dec, dec], axis=0)
                ah = ah + masks_ref[l] * _dot_nt(prod[:chunk], prod[chunk:])
            a.append(ah)
        for h in range(heads):
            out_sc[h] = _dot(a[h].astype(BF16), aux_sc[h, chunk:, :]) + o_inter[h]

    def emit(c, slot):
        out_sc = slots[slot][3]
        rows = chunk_rows(c)
        for h in range(heads):
            sl = slice(h * hd, (h + 1) * hd)
            gt = gate_ref[0, rows, sl]
            o_ref[0, rows, sl] = _rms(out_sc[h], gain_all[:, sl]) * (gt * jax.nn.sigmoid(gt))

    prepare(0, 0)
    prepare(1, 1)
    mix(0)

    def steady(j, carry):
        c = 2 * j + 1
        emit(c - 1, 0)
        prepare(c + 1, 0)
        mix(1)
        emit(c, 1)
        prepare(c + 2, 1)
        mix(0)
        return carry

    lax.fori_loop(0, (nchunks - 2) // 2, steady, 0, unroll=True)
    emit(nchunks - 2, 0)
    mix(1)
    emit(nchunks - 1, 1)


def _hgrn(hg, lb_logits, out_gain, batch, seq, layer):
    width = hg.shape[-1] // 4
    heads = width // HGRN_HEAD_DIM
    chunk = HGRN_CHUNK
    tile = HGRN_SEQ_TILE
    tri, masks = _hgrn_tables(chunk)
    nrows = lb_logits.shape[0]

    def col(k):
        return pl.BlockSpec((1, tile, width), lambda b, t, k=k: (b, t, k))

    return pl.pallas_call(
        functools.partial(_hgrn_kernel, tile=tile, chunk=chunk, layer=layer, heads=heads),
        grid=(batch, seq // tile),
        in_specs=[
            col(0), col(1), col(2), col(3),
            pl.BlockSpec((nrows, width), lambda b, t: (0, 0)),
            pl.BlockSpec((1, width), lambda b, t: (0, 0)),
            pl.BlockSpec(tri.shape, lambda b, t: (0, 0)),
            pl.BlockSpec(masks.shape, lambda b, t: (0, 0, 0)),
        ],
        out_specs=pl.BlockSpec((1, tile, width), lambda b, t: (b, t, 0)),
        out_shape=jax.ShapeDtypeStruct((batch, seq, width), F32),
        scratch_shapes=(
            [pltpu.VMEM((heads, HGRN_HEAD_DIM, HGRN_HEAD_DIM), F32)]
            + 2 * [pltpu.VMEM((heads, 2 * chunk, HGRN_HEAD_DIM), BF16),
                   pltpu.VMEM((heads, 2 * chunk, HGRN_HEAD_DIM), BF16),
                   pltpu.VMEM((heads, chunk, HGRN_HEAD_DIM), F32),
                   pltpu.VMEM((heads, chunk, HGRN_HEAD_DIM), F32)]),
        compiler_params=pltpu.CompilerParams(
            dimension_semantics=("arbitrary", "arbitrary"), vmem_limit_bytes=VMEM_LIMIT_BYTES),
        name="hgrn2",
    )(hg, hg, hg, hg, lb_logits, out_gain, jnp.asarray(tri, BF16), jnp.asarray(masks, F32))


def _out_ffn_kernel(x_ref, a_ref, r_ref, ga_ref, wo_ref, gpost_ref, gpre_ref, w1_ref, w2_ref,
                    gpost2_ref, o_ref, *, ff_tile):
    aw = a_ref.shape[1]
    d_ff = w1_ref.shape[1]
    part = x_ref.shape[0] // FFN_ROW_PARTS
    halves = [pl.ds(i * part, part) for i in range(FFN_ROW_PARTS)]
    x1 = []
    for r in halves:
        an = _rms(a_ref[r, :], ga_ref[...]).astype(BF16)
        mixed = _dot(an, wo_ref[:aw, :]) + _dot(r_ref[r, :].astype(BF16), wo_ref[aw:, :])
        x1.append(x_ref[r, :] + _rms(mixed, gpost_ref[...]))
    for r, x1h in zip(halves, x1):
        h = _rms(x1h, gpre_ref[...]).astype(BF16)
        acc = jnp.zeros(x1h.shape, F32)
        for j in range(d_ff // ff_tile):
            u = _dot(h, w1_ref[:, j * ff_tile:(j + 1) * ff_tile])
            u = jnp.square(jnp.maximum(u, 0.0)).astype(BF16)
            acc = acc + _dot(u, w2_ref[j * ff_tile:(j + 1) * ff_tile, :])
        o_ref[r, :] = x1h + _rms(acc, gpost2_ref[...])


def _out_ffn(x2, attn, rec, ga, wo_b, gpost, gpre, w1_b, w2_b, gpost2):
    n, d = x2.shape
    aw = attn.shape[1]
    rw = rec.shape[1]
    d_ff = w1_b.shape[1]

    def rows(w):
        return pl.BlockSpec((FFN_ROW_TILE, w), lambda i: (i, 0))

    def whole(shape):
        return pl.BlockSpec(shape, lambda i: (0, 0), pipeline_mode=pl.Buffered(1))

    return pl.pallas_call(
        functools.partial(_out_ffn_kernel, ff_tile=FFN_HIDDEN_TILE),
        grid=(n // FFN_ROW_TILE,),
        in_specs=[
            rows(d), rows(aw), rows(rw), whole((1, aw)), whole((aw + rw, d)), whole((1, d)),
            whole((1, d)), whole((d, d_ff)), whole((d_ff, d)), whole((1, d)),
        ],
        out_specs=rows(d),
        out_shape=jax.ShapeDtypeStruct((n, d), F32),
        compiler_params=pltpu.CompilerParams(
            dimension_semantics=("arbitrary",), vmem_limit_bytes=VMEM_LIMIT_BYTES),
        name="outproj_ffn",
    )(x2, attn, rec, ga, wo_b, gpost, gpre, w1_b, w2_b, gpost2)


def kernel(x, mix_pre_norm, w_in, attn_out_norm, hgrn_lb_logits, hgrn_out_norm, w_out,
           mix_post_norm, mlp_pre_norm, w_ff1, w_ff2, mlp_post_norm):
    batch, seq, d = x.shape
    depth = w_in.shape[0]
    aw = attn_out_norm.shape[1]
    n = batch * seq
    x2 = x.reshape(n, d)
    for layer in range(depth):
        qkv, hg = _inproj(x2, mix_pre_norm[layer][None], w_in[layer].astype(BF16), 3 * aw)
        attn = _attention(qkv.reshape(batch, seq, -1), batch, seq)
        rec = _hgrn(hg.reshape(batch, seq, -1), hgrn_lb_logits, hgrn_out_norm[layer][None],
                    batch, seq, layer)
        x2 = _out_ffn(x2, attn.reshape(n, -1), rec.reshape(n, -1), attn_out_norm[layer][None],
                      w_out[layer].astype(BF16), mix_post_norm[layer][None],
                      mlp_pre_norm[layer][None], w_ff1[layer].astype(BF16),
                      w_ff2[layer].astype(BF16), mlp_post_norm[layer][None])
    return x2.reshape(batch, seq, d)
```

```python
import functools

import numpy as np
import jax
import jax.numpy as jnp
from jax import lax
from jax.experimental import pallas as pl
from jax.experimental.pallas import tpu as pltpu

F32 = jnp.float32
BF16 = jnp.bfloat16

RMS_EPS = 1e-6
ATTN_HEAD_DIM = 64
ATTN_BLOCK = 128
ATTN_GROUP = 8
DILATED_PATTERNS = ((128, 1), (512, 4), (2048, 16))
HGRN_HEAD_DIM = 128
HGRN_CHUNK = 64
HGRN_SEQ_TILE = 1024
LANES = 128
SUBLANES = 8
MASKED_SCORE = -1e30
LOG2E = 1.4426950408889634
VMEM_LIMIT_BYTES = 56 * 1024 * 1024
INPROJ_ROW_TILE = 1024
INPROJ_ROW_PARTS = 4
FFN_ROW_TILE = 1024
FFN_HIDDEN_TILE = 1024
FFN_ROW_PARTS = 4


def _rms(x, gain):
    return x * lax.rsqrt(jnp.mean(x * x, axis=-1, keepdims=True) + RMS_EPS) * gain


def _dot(a, b):
    return jnp.dot(a, b, preferred_element_type=F32)


def _dot_nt(a, b):
    return lax.dot_general(a, b, (((1,), (1,)), ((), ())), preferred_element_type=F32)


def _dot_tn(a, b):
    return lax.dot_general(a, b, (((0,), (0,)), ((), ())), preferred_element_type=F32)


def _inproj_kernel(x_ref, g_ref, w_ref, qkv_ref, hg_ref, *, attn_cols):
    part = x_ref.shape[0] // INPROJ_ROW_PARTS
    parts = [pl.ds(i * part, part) for i in range(INPROJ_ROW_PARTS)]
    normed = [_rms(x_ref[r, :], g_ref[...]).astype(BF16) for r in parts]
    for r, h in zip(parts, normed):
        qkv_ref[r, :] = _dot(h, w_ref[:, :attn_cols])
        hg_ref[r, :] = _dot(h, w_ref[:, attn_cols:])


def _inproj(x2, gain, w_in_b, attn_cols):
    n, d = x2.shape
    width = w_in_b.shape[1]
    return pl.pallas_call(
        functools.partial(_inproj_kernel, attn_cols=attn_cols),
        grid=(n // INPROJ_ROW_TILE,),
        in_specs=[
            pl.BlockSpec((INPROJ_ROW_TILE, d), lambda i: (i, 0)),
            pl.BlockSpec((1, d), lambda i: (0, 0)),
            pl.BlockSpec((d, width), lambda i: (0, 0), pipeline_mode=pl.Buffered(1)),
        ],
        out_specs=[
            pl.BlockSpec((INPROJ_ROW_TILE, attn_cols), lambda i: (i, 0)),
            pl.BlockSpec((INPROJ_ROW_TILE, width - attn_cols), lambda i: (i, 0)),
        ],
        out_shape=[
            jax.ShapeDtypeStruct((n, attn_cols), F32),
            jax.ShapeDtypeStruct((n, width - attn_cols), F32),
        ],
        compiler_params=pltpu.CompilerParams(
            dimension_semantics=("arbitrary",), vmem_limit_bytes=VMEM_LIMIT_BYTES),
        name="inproj",
    )(x2, gain, w_in_b)


def _attn_kernel(q_ref, k_ref, v_ref, o_ref, m_sc, acc_sc, den_sc, bias_sc, *, seq):
    hp = pl.program_id(1)
    blk = ATTN_BLOCK
    patterns = sorted(DILATED_PATTERNS, key=lambda p: -p[1])
    npat = len(patterns)
    lane = lax.broadcasted_iota(jnp.int32, (blk, LANES), 1)
    head0 = lane < ATTN_HEAD_DIM
    ones_tile = jnp.ones((blk, LANES), BF16)

    row = lax.broadcasted_iota(jnp.int32, (2 * blk, 2 * blk), 0)
    col = lax.broadcasted_iota(jnp.int32, (2 * blk, 2 * blk), 1)
    qi = row & (blk - 1)
    kj = col & (blk - 1)
    is_cur = col >= blk
    dist = jnp.where(is_cur, qi - kj, qi - kj + blk)
    valid = (dist >= 0) & (dist <= blk)
    slope = jnp.exp2((-(2 * hp + (row >> 7) + 1)).astype(F32))
    for bi, (_, dil) in enumerate(patterns):
        bias = -(slope * LOG2E) * (dist * dil).astype(F32)
        bias_sc[bi] = jnp.where(valid, bias, MASKED_SCORE).astype(BF16)

    def tile_rows_of(dil, run):
        start, first, count = run
        step = blk * dil

        def rows(st):
            return pl.ds(st, blk, stride=dil) if dil > 1 else pl.ds(st, blk)

        return [None if first else rows(start - step)] + [
            rows(start + g * step) for g in range(count)]

    def group(bi, dil, runs):
        blocks = []
        for run in runs:
            tile_rows = tile_rows_of(dil, run)
            keys = [None if r is None else k_ref[0, r, :].astype(BF16) for r in tile_rows]
            vals = [None if r is None else
                    jnp.concatenate([v_ref[0, r, :].astype(BF16), ones_tile], axis=1)
                    for r in tile_rows]
            for g in range(run[2]):
                blocks.append((tile_rows[g + 1], keys[g], keys[g + 1], vals[g], vals[g + 1]))
        scores = []
        for r, kp, kc, _, _ in blocks:
            qb = q_ref[0, r, :] * (ATTN_HEAD_DIM ** -0.5 * LOG2E)
            qs = jnp.concatenate([jnp.where(head0, qb, 0.0), jnp.where(head0, 0.0, qb)],
                                 axis=0).astype(BF16)
            if kp is None:
                scores.append(_dot_nt(qs, kc).astype(BF16) + bias_sc[bi, :, blk:])
            else:
                kcat = jnp.concatenate([kp, kc], axis=0)
                scores.append(_dot_nt(qs, kcat).astype(BF16) + bias_sc[bi])
        maxes = [jnp.max(s, axis=-1, keepdims=True) for s in scores]
        probs = [jnp.exp2(s - mx) for s, mx in zip(scores, maxes)]
        outs = []
        for (_, _, _, vp, vc), p in zip(blocks, probs):
            outs.append(_dot(p, vc) if vp is None else _dot(p, jnp.concatenate([vp, vc], axis=0)))
        for (r, _, _, _, _), out, mx in zip(blocks, outs, maxes):
            mx = mx.astype(F32)
            acc = jnp.where(head0, out[:blk, :LANES], out[blk:, :LANES])
            den = jnp.where(head0, out[:blk, LANES:], out[blk:, LANES:])
            mb = jnp.where(head0, jnp.broadcast_to(mx[:blk], (blk, LANES)),
                           jnp.broadcast_to(mx[blk:], (blk, LANES)))
            if bi > 0:
                m_old = m_sc[r, :]
                m_new = jnp.maximum(m_old, mb)
                w_old = jnp.exp2(m_old - m_new)
                w_new = jnp.exp2(mb - m_new)
                acc = acc_sc[r, :] * w_old + acc * w_new
                den = den_sc[r, :] * w_old + den * w_new
                mb = m_new
            if bi < npat - 1:
                m_sc[r, :] = mb
                acc_sc[r, :] = acc
                den_sc[r, :] = den
            else:
                o_ref[0, r, :] = acc / den

    for bi, (_, dil) in enumerate(patterns):
        run_len = seq // dil // blk
        for g in range(seq // blk // ATTN_GROUP):
            if run_len >= ATTN_GROUP:
                per_residue = run_len // ATTN_GROUP
                i = g % per_residue
                runs = [(g // per_residue + i * (ATTN_GROUP * blk * dil), i == 0, ATTN_GROUP)]
            else:
                nruns = ATTN_GROUP // run_len
                runs = [(g * nruns + u, True, run_len) for u in range(nruns)]
            group(bi, dil, runs)


def _attention(qkv, batch, seq):
    width = qkv.shape[-1] // 3
    pairs = width // LANES
    blk = ATTN_BLOCK
    return pl.pallas_call(
        functools.partial(_attn_kernel, seq=seq),
        grid=(batch, pairs),
        in_specs=[
            pl.BlockSpec((1, seq, LANES), lambda b, h: (b, 0, h)),
            pl.BlockSpec((1, seq, LANES), lambda b, h: (b, 0, pairs + h)),
            pl.BlockSpec((1, seq, LANES), lambda b, h: (b, 0, 2 * pairs + h)),
        ],
        out_specs=pl.BlockSpec((1, seq, LANES), lambda b, h: (b, 0, h)),
        out_shape=jax.ShapeDtypeStruct((batch, seq, width), F32),
        scratch_shapes=[
            pltpu.VMEM((seq, LANES), F32),
            pltpu.VMEM((seq, LANES), F32),
            pltpu.VMEM((seq, LANES), F32),
            pltpu.VMEM((len(DILATED_PATTERNS), 2 * blk, 2 * blk), BF16),
        ],
        compiler_params=pltpu.CompilerParams(
            dimension_semantics=("arbitrary", "arbitrary"), vmem_limit_bytes=VMEM_LIMIT_BYTES),
        name="dilated_attention",
    )(qkv, qkv, qkv)


def _hgrn_tables(chunk):
    levels = int(np.log2(chunk))
    t = np.arange(chunk)
    masks = np.zeros((levels + 1, chunk, chunk), np.float32)
    for l in range(levels):
        blk = t // (1 << l)
        masks[l] = ((blk[:, None] & 1) == 1) & (blk[None, :] == blk[:, None] - 1)
    masks[levels] = np.eye(chunk, dtype=np.float32)
    assert np.array_equal(masks.sum(0), np.tril(np.ones((chunk, chunk), np.float32)))
    return np.tril(np.ones((chunk, chunk), np.float32)), masks


def _hgrn_kernel(q_ref, f_ref, i_ref, gate_ref, lbl_ref, gain_ref, tri_ref, masks_ref,
                 o_ref, state_sc, *b_slots, tile, chunk, layer, heads):
    hd = HGRN_HEAD_DIM
    levels = masks_ref.shape[0] - 1

    @pl.when(pl.program_id(1) == 0)
    def _():
        state_sc[...] = jnp.zeros(state_sc.shape, F32)

    logits = lbl_ref[...]
    e = jnp.exp(logits - jnp.max(logits, axis=0, keepdims=True))
    lb_all = jnp.sum(e[0:layer + 1, :], axis=0, keepdims=True) / jnp.sum(e, axis=0, keepdims=True)
    gain_all = gain_ref[...]
    row = lax.broadcasted_iota(jnp.int32, (chunk, hd), 0)
    odd_row = (row & 1) == 1
    low_half = (row & 4) == 0
    tri = tri_ref[...]

    role_sign = {l: jnp.where(((row >> l) & 1) == 1, 1.0, -1.0)
                 for l in range(1, levels) if (1 << l) < SUBLANES}

    def level_exponent(b_sc, h, l):
        m = 1 << l

        def bcast(r):
            return jnp.broadcast_to(b_sc[h, r:r + 1, :], (SUBLANES, hd))
        pieces = []
        for s in range(chunk // SUBLANES):
            first = s * SUBLANES
            slab = b_sc[h, first:first + SUBLANES, :]
            if m >= SUBLANES:
                bd = bcast(first // (2 * m) * (2 * m) + m - 1)
                pieces.append(slab - bd if (first // m) % 2 == 1 else bd - slab)
            else:
                if m == 4:
                    bd = bcast(first + 3)
                else:
                    bd = jnp.where(low_half[:SUBLANES], bcast(first + 1), bcast(first + 5))
                pieces.append((slab - bd) * role_sign[l][:SUBLANES])
        return jnp.concatenate(pieces, axis=0)

    def prepare(c):
        b_sc = b_slots[c % 2]
        rows = pl.ds(c * chunk, chunk)
        qk, d0, v = [], [], []
        for h in range(heads):
            sl = slice(h * hd, (h + 1) * hd)
            lb = lb_all[:, sl]
            qr = q_ref[0, rows, sl]
            sig = jax.nn.sigmoid(f_ref[0, rows, sl])
            f = lb + (1.0 - lb) * sig
            qk.append(jnp.concatenate([qr * jax.nn.sigmoid(qr), (1.0 - lb) * (1.0 - sig)],
                                      axis=0).astype(BF16))
            d0.append(jnp.where(odd_row, f, 1.0).astype(BF16))
            v.append(i_ref[0, rows, sl].astype(BF16))
            g = jnp.log2(f)
            g_hi = g.astype(BF16)
            g_lo = (g - g_hi.astype(F32)).astype(BF16)
            b_sc[h] = _dot(tri, g_hi) + _dot(tri, g_lo)
        return qk, d0, v

    def mix(c, qk, d0, v):
        b_sc = b_slots[c % 2]
        o_inter = []
        for h in range(heads):
            b = b_sc[h]
            state_t = state_sc[h]
            q_dec = qk[h][:chunk] * jnp.exp2(b).astype(BF16)
            o_inter.append(_dot_nt(q_dec, state_t.astype(BF16)))
            b_last = b[chunk - 1:chunk, :]
            k_end = qk[h][chunk:] * jnp.exp2(b_last - b).astype(BF16)
            state_sc[h] = state_t * jnp.exp2(b_last) + _dot_tn(v[h], k_end)
        a = []
        for h in range(heads):
            ah = masks_ref[levels] * _dot_nt(qk[h][:chunk], qk[h][chunk:])
            for l in range(levels):
                dec = d0[h] if l == 0 else jnp.exp2(level_exponent(b_sc, h, l)).astype(BF16)
                prod = qk[h] * jnp.concatenate([dec, dec], axis=0)
                ah = ah + masks_ref[l] * _dot_nt(prod[:chunk], prod[chunk:])
            a.append(ah)
        return [_dot(a[h].astype(BF16), v[h]) + o_inter[h] for h in range(heads)]

    def emit(c, outs):
        rows = pl.ds(c * chunk, chunk)
        for h in range(heads):
            sl = slice(h * hd, (h + 1) * hd)
            gt = gate_ref[0, rows, sl]
            o_ref[0, rows, sl] = _rms(outs[h], gain_all[:, sl]) * (gt * jax.nn.sigmoid(gt))

    nchunks = tile // chunk
    prepared = {0: prepare(0)}
    outs = None
    for c in range(nchunks):
        if c > 0:
            emit(c - 1, outs)
        if c + 1 < nchunks:
            prepared[c + 1] = prepare(c + 1)
        outs = mix(c, *prepared.pop(c))
    emit(nchunks - 1, outs)


def _hgrn(hg, lb_logits, out_gain, batch, seq, layer):
    width = hg.shape[-1] // 4
    heads = width // HGRN_HEAD_DIM
    chunk = HGRN_CHUNK
    tile = HGRN_SEQ_TILE
    tri, masks = _hgrn_tables(chunk)
    nrows = lb_logits.shape[0]

    def col(k):
        return pl.BlockSpec((1, tile, width), lambda b, t, k=k: (b, t, k))

    return pl.pallas_call(
        functools.partial(_hgrn_kernel, tile=tile, chunk=chunk, layer=layer, heads=heads),
        grid=(batch, seq // tile),
        in_specs=[
            col(0), col(1), col(2), col(3),
            pl.BlockSpec((nrows, width), lambda b, t: (0, 0)),
            pl.BlockSpec((1, width), lambda b, t: (0, 0)),
            pl.BlockSpec(tri.shape, lambda b, t: (0, 0)),
            pl.BlockSpec(masks.shape, lambda b, t: (0, 0, 0)),
        ],
        out_specs=pl.BlockSpec((1, tile, width), lambda b, t: (b, t, 0)),
        out_shape=jax.ShapeDtypeStruct((batch, seq, width), F32),
        scratch_shapes=(
            [pltpu.VMEM((heads, HGRN_HEAD_DIM, HGRN_HEAD_DIM), F32)]
            + 2 * [pltpu.VMEM((heads, chunk, HGRN_HEAD_DIM), F32)]),
        compiler_params=pltpu.CompilerParams(
            dimension_semantics=("arbitrary", "arbitrary"), vmem_limit_bytes=VMEM_LIMIT_BYTES),
        name="hgrn2",
    )(hg, hg, hg, hg, lb_logits, out_gain, jnp.asarray(tri, BF16), jnp.asarray(masks, F32))


def _out_ffn_kernel(x_ref, a_ref, r_ref, ga_ref, wo_ref, gpost_ref, gpre_ref, w1_ref, w2_ref,
                    gpost2_ref, o_ref, *, ff_tile):
    aw = a_ref.shape[1]
    d_ff = w1_ref.shape[1]
    part = x_ref.shape[0] // FFN_ROW_PARTS
    halves = [pl.ds(i * part, part) for i in range(FFN_ROW_PARTS)]
    x1 = []
    for r in halves:
        an = _rms(a_ref[r, :], ga_ref[...]).astype(BF16)
        mixed = _dot(an, wo_ref[:aw, :]) + _dot(r_ref[r, :].astype(BF16), wo_ref[aw:, :])
        x1.append(x_ref[r, :] + _rms(mixed, gpost_ref[...]))
    for r, x1h in zip(halves, x1):
        h = _rms(x1h, gpre_ref[...]).astype(BF16)
        acc = jnp.zeros(x1h.shape, F32)
        for j in range(d_ff // ff_tile):
            u = _dot(h, w1_ref[:, j * ff_tile:(j + 1) * ff_tile])
            u = jnp.square(jnp.maximum(u, 0.0)).astype(BF16)
            acc = acc + _dot(u, w2_ref[j * ff_tile:(j + 1) * ff_tile, :])
        o_ref[r, :] = x1h + _rms(acc, gpost2_ref[...])


def _out_ffn(x2, attn, rec, ga, wo_b, gpost, gpre, w1_b, w2_b, gpost2):
    n, d = x2.shape
    aw = attn.shape[1]
    rw = rec.shape[1]
    d_ff = w1_b.shape[1]

    def rows(w):
        return pl.BlockSpec((FFN_ROW_TILE, w), lambda i: (i, 0))

    def whole(shape):
        return pl.BlockSpec(shape, lambda i: (0, 0), pipeline_mode=pl.Buffered(1))

    return pl.pallas_call(
        functools.partial(_out_ffn_kernel, ff_tile=FFN_HIDDEN_TILE),
        grid=(n // FFN_ROW_TILE,),
        in_specs=[
            rows(d), rows(aw), rows(rw), whole((1, aw)), whole((aw + rw, d)), whole((1, d)),
            whole((1, d)), whole((d, d_ff)), whole((d_ff, d)), whole((1, d)),
        ],
        out_specs=rows(d),
        out_shape=jax.ShapeDtypeStruct((n, d), F32),
        compiler_params=pltpu.CompilerParams(
            dimension_semantics=("arbitrary",), vmem_limit_bytes=VMEM_LIMIT_BYTES),
        name="outproj_ffn",
    )(x2, attn, rec, ga, wo_b, gpost, gpre, w1_b, w2_b, gpost2)


def kernel(x, mix_pre_norm, w_in, attn_out_norm, hgrn_lb_logits, hgrn_out_norm, w_out,
           mix_post_norm, mlp_pre_norm, w_ff1, w_ff2, mlp_post_norm):
    batch, seq, d = x.shape
    depth = w_in.shape[0]
    aw = attn_out_norm.shape[1]
    n = batch * seq
    x2 = x.reshape(n, d)
    for layer in range(depth):
        qkv, hg = _inproj(x2, mix_pre_norm[layer][None], w_in[layer].astype(BF16), 3 * aw)
        attn = _attention(qkv.reshape(batch, seq, -1), batch, seq)
        rec = _hgrn(hg.reshape(batch, seq, -1), hgrn_lb_logits, hgrn_out_norm[layer][None],
                    batch, seq, layer)
        x2 = _out_ffn(x2, attn.reshape(n, -1), rec.reshape(n, -1), attn_out_norm[layer][None],
                      w_out[layer].astype(BF16), mix_post_norm[layer][None],
                      mlp_pre_norm[layer][None], w_ff1[layer].astype(BF16),
                      w_ff2[layer].astype(BF16), mlp_post_norm[layer][None])
    return x2.reshape(batch, seq, d)
```

```python
import functools

import numpy as np
import jax
import jax.numpy as jnp
from jax import lax
from jax.experimental import pallas as pl
from jax.experimental.pallas import tpu as pltpu

F32 = jnp.float32
BF16 = jnp.bfloat16

RMS_EPS = 1e-6
ATTN_HEAD_DIM = 64
ATTN_BLOCK = 128
ATTN_GROUP = 8
DILATED_PATTERNS = ((128, 1), (512, 4), (2048, 16))
HGRN_HEAD_DIM = 128
HGRN_CHUNK = 64
HGRN_SEQ_TILE = 1024
LANES = 128
SUBLANES = 8
MASKED_SCORE = -1e30
LOG2E = 1.4426950408889634
VMEM_LIMIT_BYTES = 56 * 1024 * 1024
INPROJ_ROW_TILE = 1024
INPROJ_ROW_PARTS = 4
FFN_ROW_TILE = 1024
FFN_HIDDEN_TILE = 1024
FFN_ROW_PARTS = 4


def _rms(x, gain):
    return x * lax.rsqrt(jnp.mean(x * x, axis=-1, keepdims=True) + RMS_EPS) * gain


def _dot(a, b):
    return jnp.dot(a, b, preferred_element_type=F32)


def _dot_nt(a, b):
    return lax.dot_general(a, b, (((1,), (1,)), ((), ())), preferred_element_type=F32)


def _dot_tn(a, b):
    return lax.dot_general(a, b, (((0,), (0,)), ((), ())), preferred_element_type=F32)


def _inproj_kernel(x_ref, g_ref, w_ref, qkv_ref, hg_ref, *, attn_cols):
    part = x_ref.shape[0] // INPROJ_ROW_PARTS
    parts = [pl.ds(i * part, part) for i in range(INPROJ_ROW_PARTS)]
    normed = [_rms(x_ref[r, :], g_ref[...]).astype(BF16) for r in parts]
    for r, h in zip(parts, normed):
        qkv_ref[r, :] = _dot(h, w_ref[:, :attn_cols])
        hg_ref[r, :] = _dot(h, w_ref[:, attn_cols:])


def _inproj(x2, gain, w_in_b, attn_cols):
    n, d = x2.shape
    width = w_in_b.shape[1]
    return pl.pallas_call(
        functools.partial(_inproj_kernel, attn_cols=attn_cols),
        grid=(n // INPROJ_ROW_TILE,),
        in_specs=[
            pl.BlockSpec((INPROJ_ROW_TILE, d), lambda i: (i, 0)),
            pl.BlockSpec((1, d), lambda i: (0, 0)),
            pl.BlockSpec((d, width), lambda i: (0, 0), pipeline_mode=pl.Buffered(1)),
        ],
        out_specs=[
            pl.BlockSpec((INPROJ_ROW_TILE, attn_cols), lambda i: (i, 0)),
            pl.BlockSpec((INPROJ_ROW_TILE, width - attn_cols), lambda i: (i, 0)),
        ],
        out_shape=[
            jax.ShapeDtypeStruct((n, attn_cols), F32),
            jax.ShapeDtypeStruct((n, width - attn_cols), F32),
        ],
        compiler_params=pltpu.CompilerParams(
            dimension_semantics=("arbitrary",), vmem_limit_bytes=VMEM_LIMIT_BYTES),
        name="inproj",
    )(x2, gain, w_in_b)


def _attn_kernel(q_ref, k_ref, v_ref, o_ref, m_sc, acc_sc, den_sc, bias_sc, *, seq):
    hp = pl.program_id(1)
    blk = ATTN_BLOCK
    patterns = sorted(DILATED_PATTERNS, key=lambda p: -p[1])
    npat = len(patterns)
    lane = lax.broadcasted_iota(jnp.int32, (blk, LANES), 1)
    head0 = lane < ATTN_HEAD_DIM
    ones_tile = jnp.ones((blk, LANES), BF16)

    row = lax.broadcasted_iota(jnp.int32, (2 * blk, 2 * blk), 0)
    col = lax.broadcasted_iota(jnp.int32, (2 * blk, 2 * blk), 1)
    qi = row & (blk - 1)
    kj = col & (blk - 1)
    is_cur = col >= blk
    dist = jnp.where(is_cur, qi - kj, qi - kj + blk)
    valid = (dist >= 0) & (dist <= blk)
    slope = jnp.exp2((-(2 * hp + (row >> 7) + 1)).astype(F32))
    for bi, (_, dil) in enumerate(patterns):
        bias = -(slope * LOG2E) * (dist * dil).astype(F32)
        bias_sc[bi] = jnp.where(valid, bias, MASKED_SCORE).astype(BF16)

    def tile_rows_of(dil, run):
        start, first, count = run
        step = blk * dil

        def rows(st):
            return pl.ds(st, blk, stride=dil) if dil > 1 else pl.ds(st, blk)

        return [None if first else rows(start - step)] + [
            rows(start + g * step) for g in range(count)]

    def group(bi, dil, runs):
        blocks = []
        for run in runs:
            tile_rows = tile_rows_of(dil, run)
            keys = [None if r is None else k_ref[0, r, :].astype(BF16) for r in tile_rows]
            vals = [None if r is None else
                    jnp.concatenate([v_ref[0, r, :].astype(BF16), ones_tile], axis=1)
                    for r in tile_rows]
            for g in range(run[2]):
                blocks.append((tile_rows[g + 1], keys[g], keys[g + 1], vals[g], vals[g + 1]))
        scores = []
        for r, kp, kc, _, _ in blocks:
            qb = q_ref[0, r, :] * (ATTN_HEAD_DIM ** -0.5 * LOG2E)
            qs = jnp.concatenate([jnp.where(head0, qb, 0.0), jnp.where(head0, 0.0, qb)],
                                 axis=0).astype(BF16)
            if kp is None:
                scores.append(_dot_nt(qs, kc).astype(BF16) + bias_sc[bi, :, blk:])
            else:
                kcat = jnp.concatenate([kp, kc], axis=0)
                scores.append(_dot_nt(qs, kcat).astype(BF16) + bias_sc[bi])
        maxes = [jnp.max(s, axis=-1, keepdims=True) for s in scores]
        probs = [jnp.exp2(s - mx) for s, mx in zip(scores, maxes)]
        outs = []
        for (_, _, _, vp, vc), p in zip(blocks, probs):
            outs.append(_dot(p, vc) if vp is None else _dot(p, jnp.concatenate([vp, vc], axis=0)))
        for (r, _, _, _, _), out, mx in zip(blocks, outs, maxes):
            mx = mx.astype(F32)
            acc = jnp.where(head0, out[:blk, :LANES], out[blk:, :LANES])
            den = jnp.where(head0, out[:blk, LANES:], out[blk:, LANES:])
            mb = jnp.where(head0, jnp.broadcast_to(mx[:blk], (blk, LANES)),
                           jnp.broadcast_to(mx[blk:], (blk, LANES)))
            if bi > 0:
                m_old = m_sc[r, :]
                m_new = jnp.maximum(m_old, mb)
                w_old = jnp.exp2(m_old - m_new)
                w_new = jnp.exp2(mb - m_new)
                acc = acc_sc[r, :] * w_old + acc * w_new
                den = den_sc[r, :] * w_old + den * w_new
                mb = m_new
            if bi < npat - 1:
                m_sc[r, :] = mb
                acc_sc[r, :] = acc
                den_sc[r, :] = den
            else:
                o_ref[0, r, :] = acc / den

    for bi, (_, dil) in enumerate(patterns):
        run_len = seq // dil // blk
        for g in range(seq // blk // ATTN_GROUP):
            if run_len >= ATTN_GROUP:
                per_residue = run_len // ATTN_GROUP
                i = g % per_residue
                runs = [(g // per_residue + i * (ATTN_GROUP * blk * dil), i == 0, ATTN_GROUP)]
            else:
                nruns = ATTN_GROUP // run_len
                runs = [(g * nruns + u, True, run_len) for u in range(nruns)]
            group(bi, dil, runs)


def _attention(qkv, batch, seq):
    width = qkv.shape[-1] // 3
    pairs = width // LANES
    blk = ATTN_BLOCK
    return pl.pallas_call(
        functools.partial(_attn_kernel, seq=seq),
        grid=(batch, pairs),
        in_specs=[
            pl.BlockSpec((1, seq, LANES), lambda b, h: (b, 0, h)),
            pl.BlockSpec((1, seq, LANES), lambda b, h: (b, 0, pairs + h)),
            pl.BlockSpec((1, seq, LANES), lambda b, h: (b, 0, 2 * pairs + h)),
        ],
        out_specs=pl.BlockSpec((1, seq, LANES), lambda b, h: (b, 0, h)),
        out_shape=jax.ShapeDtypeStruct((batch, seq, width), F32),
        scratch_shapes=[
            pltpu.VMEM((seq, LANES), F32),
            pltpu.VMEM((seq, LANES), F32),
            pltpu.VMEM((seq, LANES), F32),
            pltpu.VMEM((len(DILATED_PATTERNS), 2 * blk, 2 * blk), BF16),
        ],
        compiler_params=pltpu.CompilerParams(
            dimension_semantics=("arbitrary", "arbitrary"), vmem_limit_bytes=VMEM_LIMIT_BYTES),
        name="dilated_attention",
    )(qkv, qkv, qkv)


def _hgrn_tables(chunk):
    levels = int(np.log2(chunk))
    t = np.arange(chunk)
    masks = np.zeros((levels + 1, chunk, chunk), np.float32)
    for l in range(levels):
        blk = t // (1 << l)
        masks[l] = ((blk[:, None] & 1) == 1) & (blk[None, :] == blk[:, None] - 1)
    masks[levels] = np.eye(chunk, dtype=np.float32)
    assert np.array_equal(masks.sum(0), np.tril(np.ones((chunk, chunk), np.float32)))
    return np.tril(np.ones((chunk, chunk), np.float32)), masks


def _hgrn_kernel(q_ref, f_ref, i_ref, gate_ref, lbl_ref, gain_ref, tri_ref, masks_ref,
                 o_ref, state_sc, *b_slots, tile, chunk, layer, heads):
    hd = HGRN_HEAD_DIM
    levels = masks_ref.shape[0] - 1

    @pl.when(pl.program_id(1) == 0)
    def _():
        state_sc[...] = jnp.zeros(state_sc.shape, F32)

    logits = lbl_ref[...]
    e = jnp.exp(logits - jnp.max(logits, axis=0, keepdims=True))
    lb_all = jnp.sum(e[0:layer + 1, :], axis=0, keepdims=True) / jnp.sum(e, axis=0, keepdims=True)
    gain_all = gain_ref[...]
    row = lax.broadcasted_iota(jnp.int32, (chunk, hd), 0)
    odd_row = (row & 1) == 1
    low_half = (row & 4) == 0
    tri = tri_ref[...]

    role_sign = {l: jnp.where(((row >> l) & 1) == 1, 1.0, -1.0)
                 for l in range(1, levels) if (1 << l) < SUBLANES}

    def level_exponent(b_sc, h, l):
        m = 1 << l

        def bcast(r):
            return jnp.broadcast_to(b_sc[h, r:r + 1, :], (SUBLANES, hd))
        pieces = []
        for s in range(chunk // SUBLANES):
            first = s * SUBLANES
            slab = b_sc[h, first:first + SUBLANES, :]
            if m >= SUBLANES:
                bd = bcast(first // (2 * m) * (2 * m) + m - 1)
                pieces.append(slab - bd if (first // m) % 2 == 1 else bd - slab)
            else:
                if m == 4:
                    bd = bcast(first + 3)
                else:
                    bd = jnp.where(low_half[:SUBLANES], bcast(first + 1), bcast(first + 5))
                pieces.append((slab - bd) * role_sign[l][:SUBLANES])
        return jnp.concatenate(pieces, axis=0)

    def prepare(c):
        b_sc = b_slots[c % 2]
        rows = pl.ds(c * chunk, chunk)
        qk, d0, v = [], [], []
        for h in range(heads):
            sl = slice(h * hd, (h + 1) * hd)
            lb = lb_all[:, sl]
            qr = q_ref[0, rows, sl]
            sig = jax.nn.sigmoid(f_ref[0, rows, sl])
            f = lb + (1.0 - lb) * sig
            qk.append(jnp.concatenate([qr * jax.nn.sigmoid(qr), (1.0 - lb) * (1.0 - sig)],
                                      axis=0).astype(BF16))
            d0.append(jnp.where(odd_row, f, 1.0).astype(BF16))
            v.append(i_ref[0, rows, sl].astype(BF16))
            g = jnp.log2(f)
            g_hi = g.astype(BF16)
            g_lo = (g - g_hi.astype(F32)).astype(BF16)
            b_sc[h] = _dot(tri, g_hi) + _dot(tri, g_lo)
        return qk, d0, v

    def mix(c, qk, d0, v):
        b_sc = b_slots[c % 2]
        o_inter = []
        for h in range(heads):
            b = b_sc[h]
            state_t = state_sc[h]
            q_dec = qk[h][:chunk] * jnp.exp2(b).astype(BF16)
            o_inter.append(_dot_nt(q_dec, state_t.astype(BF16)))
            b_last = b[chunk - 1:chunk, :]
            k_end = qk[h][chunk:] * jnp.exp2(b_last - b).astype(BF16)
            state_sc[h] = state_t * jnp.exp2(b_last) + _dot_tn(v[h], k_end)
        a = []
        for h in range(heads):
            ah = masks_ref[levels] * _dot_nt(qk[h][:chunk], qk[h][chunk:])
            for l in range(levels):
                dec = d0[h] if l == 0 else jnp.exp2(level_exponent(b_sc, h, l)).astype(BF16)
                prod = qk[h] * jnp.concatenate([dec, dec], axis=0)
                ah = ah + masks_ref[l] * _dot_nt(prod[:chunk], prod[chunk:])
            a.append(ah)
        return a, v, o_inter

    def finish(a, v, o_inter):
        return [_dot(a[h].astype(BF16), v[h]) + o_inter[h] for h in range(heads)]

    def emit(c, outs):
        rows = pl.ds(c * chunk, chunk)
        for h in range(heads):
            sl = slice(h * hd, (h + 1) * hd)
            gt = gate_ref[0, rows, sl]
            o_ref[0, rows, sl] = _rms(outs[h], gain_all[:, sl]) * (gt * jax.nn.sigmoid(gt))

    nchunks = tile // chunk
    prepared = {0: prepare(0)}
    mixed, outs = {}, {}
    for c in range(nchunks + 2):
        if c - 2 in outs:
            emit(c - 2, outs.pop(c - 2))
        if c + 1 < nchunks:
            prepared[c + 1] = prepare(c + 1)
        if c < nchunks:
            mixed[c] = mix(c, *prepared.pop(c))
        if c - 1 in mixed:
            outs[c - 1] = finish(*mixed.pop(c - 1))


def _hgrn(hg, lb_logits, out_gain, batch, seq, layer):
    width = hg.shape[-1] // 4
    heads = width // HGRN_HEAD_DIM
    chunk = HGRN_CHUNK
    tile = HGRN_SEQ_TILE
    tri, masks = _hgrn_tables(chunk)
    nrows = lb_logits.shape[0]

    def col(k):
        return pl.BlockSpec((1, tile, width), lambda b, t, k=k: (b, t, k))

    return pl.pallas_call(
        functools.partial(_hgrn_kernel, tile=tile, chunk=chunk, layer=layer, heads=heads),
        grid=(batch, seq // tile),
        in_specs=[
            col(0), col(1), col(2), col(3),
            pl.BlockSpec((nrows, width), lambda b, t: (0, 0)),
            pl.BlockSpec((1, width), lambda b, t: (0, 0)),
            pl.BlockSpec(tri.shape, lambda b, t: (0, 0)),
            pl.BlockSpec(masks.shape, lambda b, t: (0, 0, 0)),
        ],
        out_specs=pl.BlockSpec((1, tile, width), lambda b, t: (b, t, 0)),
        out_shape=jax.ShapeDtypeStruct((batch, seq, width), F32),
        scratch_shapes=(
            [pltpu.VMEM((heads, HGRN_HEAD_DIM, HGRN_HEAD_DIM), F32)]
            + 2 * [pltpu.VMEM((heads, chunk, HGRN_HEAD_DIM), F32)]),
        compiler_params=pltpu.CompilerParams(
            dimension_semantics=("arbitrary", "arbitrary"), vmem_limit_bytes=VMEM_LIMIT_BYTES),
        name="hgrn2",
    )(hg, hg, hg, hg, lb_logits, out_gain, jnp.asarray(tri, BF16), jnp.asarray(masks, F32))


def _out_ffn_kernel(x_ref, a_ref, r_ref, ga_ref, wo_ref, gpost_ref, gpre_ref, w1_ref, w2_ref,
                    gpost2_ref, o_ref, *, ff_tile):
    aw = a_ref.shape[1]
    d_ff = w1_ref.shape[1]
    part = x_ref.shape[0] // FFN_ROW_PARTS
    halves = [pl.ds(i * part, part) for i in range(FFN_ROW_PARTS)]
    x1 = []
    for r in halves:
        an = _rms(a_ref[r, :], ga_ref[...]).astype(BF16)
        mixed = _dot(an, wo_ref[:aw, :]) + _dot(r_ref[r, :].astype(BF16), wo_ref[aw:, :])
        x1.append(x_ref[r, :] + _rms(mixed, gpost_ref[...]))
    for r, x1h in zip(halves, x1):
        h = _rms(x1h, gpre_ref[...]).astype(BF16)
        acc = jnp.zeros(x1h.shape, F32)
        for j in range(d_ff // ff_tile):
            u = _dot(h, w1_ref[:, j * ff_tile:(j + 1) * ff_tile])
            u = jnp.square(jnp.maximum(u, 0.0)).astype(BF16)
            acc = acc + _dot(u, w2_ref[j * ff_tile:(j + 1) * ff_tile, :])
        o_ref[r, :] = x1h + _rms(acc, gpost2_ref[...])


def _out_ffn(x2, attn, rec, ga, wo_b, gpost, gpre, w1_b, w2_b, gpost2):
    n, d = x2.shape
    aw = attn.shape[1]
    rw = rec.shape[1]
    d_ff = w1_b.shape[1]

    def rows(w):
        return pl.BlockSpec((FFN_ROW_TILE, w), lambda i: (i, 0))

    def whole(shape):
        return pl.BlockSpec(shape, lambda i: (0, 0), pipeline_mode=pl.Buffered(1))

    return pl.pallas_call(
        functools.partial(_out_ffn_kernel, ff_tile=FFN_HIDDEN_TILE),
        grid=(n // FFN_ROW_TILE,),
        in_specs=[
            rows(d), rows(aw), rows(rw), whole((1, aw)), whole((aw + rw, d)), whole((1, d)),
            whole((1, d)), whole((d, d_ff)), whole((d_ff, d)), whole((1, d)),
        ],
        out_specs=rows(d),
        out_shape=jax.ShapeDtypeStruct((n, d), F32),
        compiler_params=pltpu.CompilerParams(
            dimension_semantics=("arbitrary",), vmem_limit_bytes=VMEM_LIMIT_BYTES),
        name="outproj_ffn",
    )(x2, attn, rec, ga, wo_b, gpost, gpre, w1_b, w2_b, gpost2)


def kernel(x, mix_pre_norm, w_in, attn_out_norm, hgrn_lb_logits, hgrn_out_norm, w_out,
           mix_post_norm, mlp_pre_norm, w_ff1, w_ff2, mlp_post_norm):
    batch, seq, d = x.shape
    depth = w_in.shape[0]
    aw = attn_out_norm.shape[1]
    n = batch * seq
    x2 = x.reshape(n, d)
    for layer in range(depth):
        qkv, hg = _inproj(x2, mix_pre_norm[layer][None], w_in[layer].astype(BF16), 3 * aw)
        attn = _attention(qkv.reshape(batch, seq, -1), batch, seq)
        rec = _hgrn(hg.reshape(batch, seq, -1), hgrn_lb_logits, hgrn_out_norm[layer][None],
                    batch, seq, layer)
        x2 = _out_ffn(x2, attn.reshape(n, -1), rec.reshape(n, -1), attn_out_norm[layer][None],
                      w_out[layer].astype(BF16), mix_post_norm[layer][None],
                      mlp_pre_norm[layer][None], w_ff1[layer].astype(BF16),
                      w_ff2[layer].astype(BF16), mlp_post_norm[layer][None])
    return x2.reshape(batch, seq, d)
```

```python
import functools

import numpy as np
import jax
import jax.numpy as jnp
from jax import lax
from jax.experimental import pallas as pl
from jax.experimental.pallas import tpu as pltpu

F32 = jnp.float32
BF16 = jnp.bfloat16

RMS_EPS = 1e-6
ATTN_HEAD_DIM = 64
ATTN_BLOCK = 128
ATTN_GROUP = 8
DILATED_PATTERNS = ((128, 1), (512, 4), (2048, 16))
HGRN_HEAD_DIM = 128
HGRN_CHUNK = 64
HGRN_SEQ_TILE = 2048
LANES = 128
SUBLANES = 8
MASKED_SCORE = -1e30
LOG2E = 1.4426950408889634
VMEM_LIMIT_BYTES = 56 * 1024 * 1024
INPROJ_ROW_TILE = 1024
INPROJ_ROW_PARTS = 4
FFN_ROW_TILE = 1024
FFN_HIDDEN_TILE = 1024
FFN_ROW_PARTS = 4


def _rms(x, gain):
    return x * lax.rsqrt(jnp.mean(x * x, axis=-1, keepdims=True) + RMS_EPS) * gain


def _dot(a, b):
    return jnp.dot(a, b, preferred_element_type=F32)


def _dot_nt(a, b):
    return lax.dot_general(a, b, (((1,), (1,)), ((), ())), preferred_element_type=F32)


def _dot_tn(a, b):
    return lax.dot_general(a, b, (((0,), (0,)), ((), ())), preferred_element_type=F32)


def _inproj_kernel(x_ref, g_ref, w_ref, qkv_ref, hg_ref, *, attn_cols):
    part = x_ref.shape[0] // INPROJ_ROW_PARTS
    parts = [pl.ds(i * part, part) for i in range(INPROJ_ROW_PARTS)]
    normed = [_rms(x_ref[r, :], g_ref[...]).astype(BF16) for r in parts]
    for r, h in zip(parts, normed):
        qkv_ref[r, :] = _dot(h, w_ref[:, :attn_cols])
        hg_ref[r, :] = _dot(h, w_ref[:, attn_cols:])


def _inproj(x2, gain, w_in_b, attn_cols):
    n, d = x2.shape
    width = w_in_b.shape[1]
    return pl.pallas_call(
        functools.partial(_inproj_kernel, attn_cols=attn_cols),
        grid=(n // INPROJ_ROW_TILE,),
        in_specs=[
            pl.BlockSpec((INPROJ_ROW_TILE, d), lambda i: (i, 0)),
            pl.BlockSpec((1, d), lambda i: (0, 0)),
            pl.BlockSpec((d, width), lambda i: (0, 0), pipeline_mode=pl.Buffered(1)),
        ],
        out_specs=[
            pl.BlockSpec((INPROJ_ROW_TILE, attn_cols), lambda i: (i, 0)),
            pl.BlockSpec((INPROJ_ROW_TILE, width - attn_cols), lambda i: (i, 0)),
        ],
        out_shape=[
            jax.ShapeDtypeStruct((n, attn_cols), F32),
            jax.ShapeDtypeStruct((n, width - attn_cols), F32),
        ],
        compiler_params=pltpu.CompilerParams(
            dimension_semantics=("arbitrary",), vmem_limit_bytes=VMEM_LIMIT_BYTES),
        name="inproj",
    )(x2, gain, w_in_b)


def _attn_kernel(q_ref, k_ref, v_ref, o_ref, m_sc, acc_sc, den_sc, bias_sc, *, seq):
    hp = pl.program_id(1)
    blk = ATTN_BLOCK
    patterns = sorted(DILATED_PATTERNS, key=lambda p: -p[1])
    npat = len(patterns)
    lane = lax.broadcasted_iota(jnp.int32, (blk, LANES), 1)
    head0 = lane < ATTN_HEAD_DIM
    ones_tile = jnp.ones((blk, LANES), BF16)

    row = lax.broadcasted_iota(jnp.int32, (2 * blk, 2 * blk), 0)
    col = lax.broadcasted_iota(jnp.int32, (2 * blk, 2 * blk), 1)
    qi = row & (blk - 1)
    kj = col & (blk - 1)
    is_cur = col >= blk
    dist = jnp.where(is_cur, qi - kj, qi - kj + blk)
    valid = (dist >= 0) & (dist <= blk)
    slope = jnp.exp2((-(2 * hp + (row >> 7) + 1)).astype(F32))
    for bi, (_, dil) in enumerate(patterns):
        bias = -(slope * LOG2E) * (dist * dil).astype(F32)
        bias_sc[bi] = jnp.where(valid, bias, MASKED_SCORE).astype(BF16)

    def tile_rows_of(dil, run):
        start, first, count = run
        step = blk * dil

        def rows(st):
            return pl.ds(st, blk, stride=dil) if dil > 1 else pl.ds(st, blk)

        return [None if first else rows(start - step)] + [
            rows(start + g * step) for g in range(count)]

    def group(bi, dil, runs):
        blocks = []
        for run in runs:
            tile_rows = tile_rows_of(dil, run)
            keys = [None if r is None else k_ref[0, r, :].astype(BF16) for r in tile_rows]
            vals = [None if r is None else
                    jnp.concatenate([v_ref[0, r, :].astype(BF16), ones_tile], axis=1)
                    for r in tile_rows]
            for g in range(run[2]):
                blocks.append((tile_rows[g + 1], keys[g], keys[g + 1], vals[g], vals[g + 1]))
        scores = []
        for r, kp, kc, _, _ in blocks:
            qb = q_ref[0, r, :] * (ATTN_HEAD_DIM ** -0.5 * LOG2E)
            qs = jnp.concatenate([jnp.where(head0, qb, 0.0), jnp.where(head0, 0.0, qb)],
                                 axis=0).astype(BF16)
            if kp is None:
                scores.append(_dot_nt(qs, kc).astype(BF16) + bias_sc[bi, :, blk:])
            else:
                kcat = jnp.concatenate([kp, kc], axis=0)
                scores.append(_dot_nt(qs, kcat).astype(BF16) + bias_sc[bi])
        maxes = [jnp.max(s, axis=-1, keepdims=True) for s in scores]
        probs = [jnp.exp2(s - mx) for s, mx in zip(scores, maxes)]
        outs = []
        for (_, _, _, vp, vc), p in zip(blocks, probs):
            outs.append(_dot(p, vc) if vp is None else _dot(p, jnp.concatenate([vp, vc], axis=0)))
        for (r, _, _, _, _), out, mx in zip(blocks, outs, maxes):
            mx = mx.astype(F32)
            acc = jnp.where(head0, out[:blk, :LANES], out[blk:, :LANES])
            den = jnp.where(head0, out[:blk, LANES:], out[blk:, LANES:])
            mb = jnp.where(head0, jnp.broadcast_to(mx[:blk], (blk, LANES)),
                           jnp.broadcast_to(mx[blk:], (blk, LANES)))
            if bi > 0:
                m_old = m_sc[r, :]
                m_new = jnp.maximum(m_old, mb)
                w_old = jnp.exp2(m_old - m_new)
                w_new = jnp.exp2(mb - m_new)
                acc = acc_sc[r, :] * w_old + acc * w_new
                den = den_sc[r, :] * w_old + den * w_new
                mb = m_new
            if bi < npat - 1:
                m_sc[r, :] = mb
                acc_sc[r, :] = acc
                den_sc[r, :] = den
            else:
                o_ref[0, r, :] = acc / den

    for bi, (_, dil) in enumerate(patterns):
        run_len = seq // dil // blk
        for g in range(seq // blk // ATTN_GROUP):
            if run_len >= ATTN_GROUP:
                per_residue = run_len // ATTN_GROUP
                i = g % per_residue
                runs = [(g // per_residue + i * (ATTN_GROUP * blk * dil), i == 0, ATTN_GROUP)]
            else:
                nruns = ATTN_GROUP // run_len
                runs = [(g * nruns + u, True, run_len) for u in range(nruns)]
            group(bi, dil, runs)


def _attention(qkv, batch, seq):
    width = qkv.shape[-1] // 3
    pairs = width // LANES
    blk = ATTN_BLOCK
    return pl.pallas_call(
        functools.partial(_attn_kernel, seq=seq),
        grid=(batch, pairs),
        in_specs=[
            pl.BlockSpec((1, seq, LANES), lambda b, h: (b, 0, h)),
            pl.BlockSpec((1, seq, LANES), lambda b, h: (b, 0, pairs + h)),
            pl.BlockSpec((1, seq, LANES), lambda b, h: (b, 0, 2 * pairs + h)),
        ],
        out_specs=pl.BlockSpec((1, seq, LANES), lambda b, h: (b, 0, h)),
        out_shape=jax.ShapeDtypeStruct((batch, seq, width), F32),
        scratch_shapes=[
            pltpu.VMEM((seq, LANES), F32),
            pltpu.VMEM((seq, LANES), F32),
            pltpu.VMEM((seq, LANES), F32),
            pltpu.VMEM((len(DILATED_PATTERNS), 2 * blk, 2 * blk), BF16),
        ],
        compiler_params=pltpu.CompilerParams(
            dimension_semantics=("arbitrary", "arbitrary"), vmem_limit_bytes=VMEM_LIMIT_BYTES),
        name="dilated_attention",
    )(qkv, qkv, qkv)


def _hgrn_tables(chunk):
    levels = int(np.log2(chunk))
    t = np.arange(chunk)
    masks = np.zeros((levels + 1, chunk, chunk), np.float32)
    for l in range(levels):
        blk = t // (1 << l)
        masks[l] = ((blk[:, None] & 1) == 1) & (blk[None, :] == blk[:, None] - 1)
    masks[levels] = np.eye(chunk, dtype=np.float32)
    assert np.array_equal(masks.sum(0), np.tril(np.ones((chunk, chunk), np.float32)))
    return np.tril(np.ones((chunk, chunk), np.float32)), masks


def _hgrn_kernel(q_ref, f_ref, i_ref, gate_ref, lbl_ref, gain_ref, tri_ref, masks_ref,
                 o_ref, state_sc, *b_slots, tile, chunk, layer, heads):
    hd = HGRN_HEAD_DIM
    levels = masks_ref.shape[0] - 1

    @pl.when(pl.program_id(1) == 0)
    def _():
        state_sc[...] = jnp.zeros(state_sc.shape, F32)

    logits = lbl_ref[...]
    e = jnp.exp(logits - jnp.max(logits, axis=0, keepdims=True))
    lb_all = jnp.sum(e[0:layer + 1, :], axis=0, keepdims=True) / jnp.sum(e, axis=0, keepdims=True)
    gain_all = gain_ref[...]
    row = lax.broadcasted_iota(jnp.int32, (chunk, hd), 0)
    odd_row = (row & 1) == 1
    low_half = (row & 4) == 0
    tri = tri_ref[...]

    role_sign = {l: jnp.where(((row >> l) & 1) == 1, 1.0, -1.0)
                 for l in range(1, levels) if (1 << l) < SUBLANES}

    def level_exponent(b_sc, h, l):
        m = 1 << l

        def bcast(r):
            return jnp.broadcast_to(b_sc[h, r:r + 1, :], (SUBLANES, hd))
        pieces = []
        for s in range(chunk // SUBLANES):
            first = s * SUBLANES
            slab = b_sc[h, first:first + SUBLANES, :]
            if m >= SUBLANES:
                bd = bcast(first // (2 * m) * (2 * m) + m - 1)
                pieces.append(slab - bd if (first // m) % 2 == 1 else bd - slab)
            else:
                if m == 4:
                    bd = bcast(first + 3)
                else:
                    bd = jnp.where(low_half[:SUBLANES], bcast(first + 1), bcast(first + 5))
                pieces.append((slab - bd) * role_sign[l][:SUBLANES])
        return jnp.concatenate(pieces, axis=0)

    def prepare(c):
        b_sc = b_slots[c % 2]
        rows = pl.ds(c * chunk, chunk)
        qk, d0, v = [], [], []
        for h in range(heads):
            sl = slice(h * hd, (h + 1) * hd)
            lb = lb_all[:, sl]
            qr = q_ref[0, rows, sl]
            sig = jax.nn.sigmoid(f_ref[0, rows, sl])
            f = lb + (1.0 - lb) * sig
            qk.append(jnp.concatenate([qr * jax.nn.sigmoid(qr), (1.0 - lb) * (1.0 - sig)],
                                      axis=0).astype(BF16))
            d0.append(jnp.where(odd_row, f, 1.0).astype(BF16))
            v.append(i_ref[0, rows, sl].astype(BF16))
            g = jnp.log2(f)
            g_hi = g.astype(BF16)
            g_lo = (g - g_hi.astype(F32)).astype(BF16)
            b_sc[h] = _dot(tri, g_hi) + _dot(tri, g_lo)
        return qk, d0, v

    def mix(c, qk, d0, v):
        b_sc = b_slots[c % 2]
        o_inter = []
        for h in range(heads):
            b = b_sc[h]
            state_t = state_sc[h]
            q_dec = qk[h][:chunk] * jnp.exp2(b).astype(BF16)
            o_inter.append(_dot_nt(q_dec, state_t.astype(BF16)))
            b_last = b[chunk - 1:chunk, :]
            k_end = qk[h][chunk:] * jnp.exp2(b_last - b).astype(BF16)
            state_sc[h] = state_t * jnp.exp2(b_last) + _dot_tn(v[h], k_end)
        a = []
        for h in range(heads):
            ah = masks_ref[levels] * _dot_nt(qk[h][:chunk], qk[h][chunk:])
            for l in range(levels):
                dec = d0[h] if l == 0 else jnp.exp2(level_exponent(b_sc, h, l)).astype(BF16)
                prod = qk[h] * jnp.concatenate([dec, dec], axis=0)
                ah = ah + masks_ref[l] * _dot_nt(prod[:chunk], prod[chunk:])
            a.append(ah)
        return a, v, o_inter

    def finish(a, v, o_inter):
        return [_dot(a[h].astype(BF16), v[h]) + o_inter[h] for h in range(heads)]

    def emit(c, outs):
        rows = pl.ds(c * chunk, chunk)
        for h in range(heads):
            sl = slice(h * hd, (h + 1) * hd)
            gt = gate_ref[0, rows, sl]
            o_ref[0, rows, sl] = _rms(outs[h], gain_all[:, sl]) * (gt * jax.nn.sigmoid(gt))

    nchunks = tile // chunk
    prepared = {0: prepare(0)}
    mixed, outs = {}, {}
    for c in range(nchunks + 2):
        if c - 2 in outs:
            emit(c - 2, outs.pop(c - 2))
        if c + 1 < nchunks:
            prepared[c + 1] = prepare(c + 1)
        if c < nchunks:
            mixed[c] = mix(c, *prepared.pop(c))
        if c - 1 in mixed:
            outs[c - 1] = finish(*mixed.pop(c - 1))


def _hgrn(hg, lb_logits, out_gain, batch, seq, layer):
    width = hg.shape[-1] // 4
    heads = width // HGRN_HEAD_DIM
    chunk = HGRN_CHUNK
    tile = HGRN_SEQ_TILE
    tri, masks = _hgrn_tables(chunk)
    nrows = lb_logits.shape[0]

    def col(k):
        return pl.BlockSpec((1, tile, width), lambda b, t, k=k: (b, t, k))

    return pl.pallas_call(
        functools.partial(_hgrn_kernel, tile=tile, chunk=chunk, layer=layer, heads=heads),
        grid=(batch, seq // tile),
        in_specs=[
            col(0), col(1), col(2), col(3),
            pl.BlockSpec((nrows, width), lambda b, t: (0, 0)),
            pl.BlockSpec((1, width), lambda b, t: (0, 0)),
            pl.BlockSpec(tri.shape, lambda b, t: (0, 0)),
            pl.BlockSpec(masks.shape, lambda b, t: (0, 0, 0)),
        ],
        out_specs=pl.BlockSpec((1, tile, width), lambda b, t: (b, t, 0)),
        out_shape=jax.ShapeDtypeStruct((batch, seq, width), F32),
        scratch_shapes=(
            [pltpu.VMEM((heads, HGRN_HEAD_DIM, HGRN_HEAD_DIM), F32)]
            + 2 * [pltpu.VMEM((heads, chunk, HGRN_HEAD_DIM), F32)]),
        compiler_params=pltpu.CompilerParams(
            dimension_semantics=("arbitrary", "arbitrary"), vmem_limit_bytes=VMEM_LIMIT_BYTES),
        name="hgrn2",
    )(hg, hg, hg, hg, lb_logits, out_gain, jnp.asarray(tri, BF16), jnp.asarray(masks, F32))


def _out_ffn_kernel(x_ref, a_ref, r_ref, ga_ref, wo_ref, gpost_ref, gpre_ref, w1_ref, w2_ref,
                    gpost2_ref, o_ref, *, ff_tile):
    aw = a_ref.shape[1]
    d_ff = w1_ref.shape[1]
    part = x_ref.shape[0] // FFN_ROW_PARTS
    halves = [pl.ds(i * part, part) for i in range(FFN_ROW_PARTS)]
    x1 = []
    for r in halves:
        an = _rms(a_ref[r, :], ga_ref[...]).astype(BF16)
        mixed = _dot(an, wo_ref[:aw, :]) + _dot(r_ref[r, :].astype(BF16), wo_ref[aw:, :])
        x1.append(x_ref[r, :] + _rms(mixed, gpost_ref[...]))
    for r, x1h in zip(halves, x1):
        h = _rms(x1h, gpre_ref[...]).astype(BF16)
        acc = jnp.zeros(x1h.shape, F32)
        for j in range(d_ff // ff_tile):
            u = _dot(h, w1_ref[:, j * ff_tile:(j + 1) * ff_tile])
            u = jnp.square(jnp.maximum(u, 0.0)).astype(BF16)
            acc = acc + _dot(u, w2_ref[j * ff_tile:(j + 1) * ff_tile, :])
        o_ref[r, :] = x1h + _rms(acc, gpost2_ref[...])


def _out_ffn(x2, attn, rec, ga, wo_b, gpost, gpre, w1_b, w2_b, gpost2):
    n, d = x2.shape
    aw = attn.shape[1]
    rw = rec.shape[1]
    d_ff = w1_b.shape[1]

    def rows(w):
        return pl.BlockSpec((FFN_ROW_TILE, w), lambda i: (i, 0))

    def whole(shape):
        return pl.BlockSpec(shape, lambda i: (0, 0), pipeline_mode=pl.Buffered(1))

    return pl.pallas_call(
        functools.partial(_out_ffn_kernel, ff_tile=FFN_HIDDEN_TILE),
        grid=(n // FFN_ROW_TILE,),
        in_specs=[
            rows(d), rows(aw), rows(rw), whole((1, aw)), whole((aw + rw, d)), whole((1, d)),
            whole((1, d)), whole((d, d_ff)), whole((d_ff, d)), whole((1, d)),
        ],
        out_specs=rows(d),
        out_shape=jax.ShapeDtypeStruct((n, d), F32),
        compiler_params=pltpu.CompilerParams(
            dimension_semantics=("arbitrary",), vmem_limit_bytes=VMEM_LIMIT_BYTES),
        name="outproj_ffn",
    )(x2, attn, rec, ga, wo_b, gpost, gpre, w1_b, w2_b, gpost2)


def kernel(x, mix_pre_norm, w_in, attn_out_norm, hgrn_lb_logits, hgrn_out_norm, w_out,
           mix_post_norm, mlp_pre_norm, w_ff1, w_ff2, mlp_post_norm):
    batch, seq, d = x.shape
    depth = w_in.shape[0]
    aw = attn_out_norm.shape[1]
    n = batch * seq
    x2 = x.reshape(n, d)
    for layer in range(depth):
        qkv, hg = _inproj(x2, mix_pre_norm[layer][None], w_in[layer].astype(BF16), 3 * aw)
        attn = _attention(qkv.reshape(batch, seq, -1), batch, seq)
        rec = _hgrn(hg.reshape(batch, seq, -1), hgrn_lb_logits, hgrn_out_norm[layer][None],
                    batch, seq, layer)
        x2 = _out_ffn(x2, attn.reshape(n, -1), rec.reshape(n, -1), attn_out_norm[layer][None],
                      w_out[layer].astype(BF16), mix_post_norm[layer][None],
                      mlp_pre_norm[layer][None], w_ff1[layer].astype(BF16),
                      w_ff2[layer].astype(BF16), mlp_post_norm[layer][None])
    return x2.reshape(batch, seq, d)
```

```python
import functools

import numpy as np
import jax
import jax.numpy as jnp
from jax import lax
from jax.experimental import pallas as pl
from jax.experimental.pallas import tpu as pltpu

F32 = jnp.float32
BF16 = jnp.bfloat16

RMS_EPS = 1e-6
ATTN_HEAD_DIM = 64
ATTN_BLOCK = 128
ATTN_GROUP = 8
DILATED_PATTERNS = ((128, 1), (512, 4), (2048, 16))
HGRN_HEAD_DIM = 128
HGRN_CHUNK = 64
HGRN_SEQ_TILE = 2048
LANES = 128
SUBLANES = 8
MASKED_SCORE = -1e30
F32_MIN_NORMAL = 1.1754944e-38
LOG2E = 1.4426950408889634
VMEM_LIMIT_BYTES = 56 * 1024 * 1024
INPROJ_ROW_TILE = 1024
INPROJ_ROW_PARTS = 4
FFN_ROW_TILE = 1024
FFN_HIDDEN_TILE = 1024
FFN_ROW_PARTS = 4


def _rms(x, gain):
    return x * lax.rsqrt(jnp.mean(x * x, axis=-1, keepdims=True) + RMS_EPS) * gain


def _dot(a, b):
    return jnp.dot(a, b, preferred_element_type=F32)


def _dot_nt(a, b):
    return lax.dot_general(a, b, (((1,), (1,)), ((), ())), preferred_element_type=F32)


def _dot_tn(a, b):
    return lax.dot_general(a, b, (((0,), (0,)), ((), ())), preferred_element_type=F32)


def _inproj_kernel(x_ref, g_ref, w_ref, qkv_ref, hg_ref, *, attn_cols):
    part = x_ref.shape[0] // INPROJ_ROW_PARTS
    parts = [pl.ds(i * part, part) for i in range(INPROJ_ROW_PARTS)]
    normed = [_rms(x_ref[r, :], g_ref[...]).astype(BF16) for r in parts]
    for r, h in zip(parts, normed):
        qkv_ref[r, :] = _dot(h, w_ref[:, :attn_cols])
        hg_ref[r, :] = _dot(h, w_ref[:, attn_cols:])


def _inproj(x2, gain, w_in_b, attn_cols):
    n, d = x2.shape
    width = w_in_b.shape[1]
    return pl.pallas_call(
        functools.partial(_inproj_kernel, attn_cols=attn_cols),
        grid=(n // INPROJ_ROW_TILE,),
        in_specs=[
            pl.BlockSpec((INPROJ_ROW_TILE, d), lambda i: (i, 0)),
            pl.BlockSpec((1, d), lambda i: (0, 0)),
            pl.BlockSpec((d, width), lambda i: (0, 0), pipeline_mode=pl.Buffered(1)),
        ],
        out_specs=[
            pl.BlockSpec((INPROJ_ROW_TILE, attn_cols), lambda i: (i, 0)),
            pl.BlockSpec((INPROJ_ROW_TILE, width - attn_cols), lambda i: (i, 0)),
        ],
        out_shape=[
            jax.ShapeDtypeStruct((n, attn_cols), F32),
            jax.ShapeDtypeStruct((n, width - attn_cols), F32),
        ],
        compiler_params=pltpu.CompilerParams(
            dimension_semantics=("arbitrary",), vmem_limit_bytes=VMEM_LIMIT_BYTES),
        name="inproj",
    )(x2, gain, w_in_b)


def _attn_kernel(q_ref, k_ref, v_ref, o_ref, m_sc, acc_sc, den_sc, bias_sc, *, seq):
    hp = pl.program_id(1)
    blk = ATTN_BLOCK
    patterns = sorted(DILATED_PATTERNS, key=lambda p: -p[1])
    npat = len(patterns)
    lane = lax.broadcasted_iota(jnp.int32, (blk, LANES), 1)
    head0 = lane < ATTN_HEAD_DIM
    ones_tile = jnp.ones((blk, LANES), BF16)

    row = lax.broadcasted_iota(jnp.int32, (2 * blk, 2 * blk), 0)
    col = lax.broadcasted_iota(jnp.int32, (2 * blk, 2 * blk), 1)
    qi = row & (blk - 1)
    kj = col & (blk - 1)
    is_cur = col >= blk
    dist = jnp.where(is_cur, qi - kj, qi - kj + blk)
    valid = (dist >= 0) & (dist <= blk)
    slope = jnp.exp2((-(2 * hp + (row >> 7) + 1)).astype(F32))
    for bi, (_, dil) in enumerate(patterns):
        bias = -(slope * LOG2E) * (dist * dil).astype(F32)
        bias_sc[bi] = jnp.where(valid, bias, MASKED_SCORE).astype(BF16)

    def tile_rows_of(dil, run):
        start, first, count = run
        step = blk * dil

        def rows(st):
            return pl.ds(st, blk, stride=dil) if dil > 1 else pl.ds(st, blk)

        return [None if first else rows(start - step)] + [
            rows(start + g * step) for g in range(count)]

    def group(bi, dil, runs):
        blocks = []
        for run in runs:
            tile_rows = tile_rows_of(dil, run)
            keys = [None if r is None else k_ref[0, r, :].astype(BF16) for r in tile_rows]
            vals = [None if r is None else
                    jnp.concatenate([v_ref[0, r, :].astype(BF16), ones_tile], axis=1)
                    for r in tile_rows]
            for g in range(run[2]):
                blocks.append((tile_rows[g + 1], keys[g], keys[g + 1], vals[g], vals[g + 1]))
        scores = []
        for r, kp, kc, _, _ in blocks:
            qb = q_ref[0, r, :] * (ATTN_HEAD_DIM ** -0.5 * LOG2E)
            qs = jnp.concatenate([jnp.where(head0, qb, 0.0), jnp.where(head0, 0.0, qb)],
                                 axis=0).astype(BF16)
            if kp is None:
                scores.append(_dot_nt(qs, kc).astype(BF16) + bias_sc[bi, :, blk:])
            else:
                kcat = jnp.concatenate([kp, kc], axis=0)
                scores.append(_dot_nt(qs, kcat).astype(BF16) + bias_sc[bi])
        maxes = [jnp.max(s, axis=-1, keepdims=True) for s in scores]
        probs = [jnp.exp2(s - mx) for s, mx in zip(scores, maxes)]
        outs = []
        for (_, _, _, vp, vc), p in zip(blocks, probs):
            outs.append(_dot(p, vc) if vp is None else _dot(p, jnp.concatenate([vp, vc], axis=0)))
        for (r, _, _, _, _), out, mx in zip(blocks, outs, maxes):
            mx = mx.astype(F32)
            acc = jnp.where(head0, out[:blk, :LANES], out[blk:, :LANES])
            den = jnp.where(head0, out[:blk, LANES:], out[blk:, LANES:])
            mb = jnp.where(head0, jnp.broadcast_to(mx[:blk], (blk, LANES)),
                           jnp.broadcast_to(mx[blk:], (blk, LANES)))
            if bi > 0:
                m_old = m_sc[r, :]
                m_new = jnp.maximum(m_old, mb)
                w_old = jnp.exp2(m_old - m_new)
                w_new = jnp.exp2(mb - m_new)
                acc = acc_sc[r, :] * w_old + acc * w_new
                den = den_sc[r, :] * w_old + den * w_new
                mb = m_new
            if bi < npat - 1:
                m_sc[r, :] = mb
                acc_sc[r, :] = acc
                den_sc[r, :] = den
            else:
                o_ref[0, r, :] = acc / den

    for bi, (_, dil) in enumerate(patterns):
        run_len = seq // dil // blk
        for g in range(seq // blk // ATTN_GROUP):
            if run_len >= ATTN_GROUP:
                per_residue = run_len // ATTN_GROUP
                i = g % per_residue
                runs = [(g // per_residue + i * (ATTN_GROUP * blk * dil), i == 0, ATTN_GROUP)]
            else:
                nruns = ATTN_GROUP // run_len
                runs = [(g * nruns + u, True, run_len) for u in range(nruns)]
            group(bi, dil, runs)


def _attention(qkv, batch, seq):
    width = qkv.shape[-1] // 3
    pairs = width // LANES
    blk = ATTN_BLOCK
    return pl.pallas_call(
        functools.partial(_attn_kernel, seq=seq),
        grid=(batch, pairs),
        in_specs=[
            pl.BlockSpec((1, seq, LANES), lambda b, h: (b, 0, h)),
            pl.BlockSpec((1, seq, LANES), lambda b, h: (b, 0, pairs + h)),
            pl.BlockSpec((1, seq, LANES), lambda b, h: (b, 0, 2 * pairs + h)),
        ],
        out_specs=pl.BlockSpec((1, seq, LANES), lambda b, h: (b, 0, h)),
        out_shape=jax.ShapeDtypeStruct((batch, seq, width), F32),
        scratch_shapes=[
            pltpu.VMEM((seq, LANES), F32),
            pltpu.VMEM((seq, LANES), F32),
            pltpu.VMEM((seq, LANES), F32),
            pltpu.VMEM((len(DILATED_PATTERNS), 2 * blk, 2 * blk), BF16),
        ],
        compiler_params=pltpu.CompilerParams(
            dimension_semantics=("arbitrary", "arbitrary"), vmem_limit_bytes=VMEM_LIMIT_BYTES),
        name="dilated_attention",
    )(qkv, qkv, qkv)


def _hgrn_tables(chunk):
    levels = int(np.log2(chunk))
    t = np.arange(chunk)
    masks = np.zeros((levels + 1, chunk, chunk), np.float32)
    for l in range(levels):
        blk = t // (1 << l)
        masks[l] = ((blk[:, None] & 1) == 1) & (blk[None, :] == blk[:, None] - 1)
    masks[levels] = np.eye(chunk, dtype=np.float32)
    assert np.array_equal(masks.sum(0), np.tril(np.ones((chunk, chunk), np.float32)))
    return np.tril(np.ones((chunk, chunk), np.float32)), masks


def _hgrn_kernel(q_ref, f_ref, i_ref, gate_ref, lbl_ref, gain_ref, tri_ref, masks_ref,
                 o_ref, state_sc, *b_slots, tile, chunk, layer, heads):
    hd = HGRN_HEAD_DIM
    levels = masks_ref.shape[0] - 1

    @pl.when(pl.program_id(1) == 0)
    def _():
        state_sc[...] = jnp.zeros(state_sc.shape, F32)

    logits = lbl_ref[...]
    e = jnp.exp(logits - jnp.max(logits, axis=0, keepdims=True))
    lb_all = jnp.sum(e[0:layer + 1, :], axis=0, keepdims=True) / jnp.sum(e, axis=0, keepdims=True)
    gain_all = gain_ref[...]
    row = lax.broadcasted_iota(jnp.int32, (chunk, hd), 0)
    odd_row = (row & 1) == 1
    low_half = (row & 4) == 0
    tri = tri_ref[...]

    role_sign = {l: jnp.where(((row >> l) & 1) == 1, 1.0, -1.0)
                 for l in range(1, levels) if (1 << l) < SUBLANES}

    def level_exponent(b_sc, h, l):
        m = 1 << l

        def bcast(r):
            return jnp.broadcast_to(b_sc[h, r:r + 1, :], (SUBLANES, hd))
        pieces = []
        for s in range(chunk // SUBLANES):
            first = s * SUBLANES
            slab = b_sc[h, first:first + SUBLANES, :]
            if m >= SUBLANES:
                bd = bcast(first // (2 * m) * (2 * m) + m - 1)
                pieces.append(slab - bd if (first // m) % 2 == 1 else bd - slab)
            else:
                if m == 4:
                    bd = bcast(first + 3)
                else:
                    bd = jnp.where(low_half[:SUBLANES], bcast(first + 1), bcast(first + 5))
                pieces.append((slab - bd) * role_sign[l][:SUBLANES])
        return jnp.concatenate(pieces, axis=0)

    def prepare(c):
        b_sc = b_slots[c % 2]
        rows = pl.ds(c * chunk, chunk)
        qk, d0, v = [], [], []
        for h in range(heads):
            sl = slice(h * hd, (h + 1) * hd)
            lb = lb_all[:, sl]
            qr = q_ref[0, rows, sl]
            sig = jax.nn.sigmoid(f_ref[0, rows, sl])
            f = lb + (1.0 - lb) * sig
            qk.append(jnp.concatenate([qr * jax.nn.sigmoid(qr), (1.0 - lb) * (1.0 - sig)],
                                      axis=0).astype(BF16))
            d0.append(jnp.where(odd_row, f, 1.0).astype(BF16))
            v.append(i_ref[0, rows, sl].astype(BF16))
            g = jnp.log2(jnp.maximum(f, F32_MIN_NORMAL))
            g_hi = g.astype(BF16)
            g_lo = (g - g_hi.astype(F32)).astype(BF16)
            b_sc[h] = _dot(tri, g_hi) + _dot(tri, g_lo)
        return qk, d0, v

    def mix(c, qk, d0, v):
        b_sc = b_slots[c % 2]
        o_inter = []
        for h in range(heads):
            b = b_sc[h]
            state_t = state_sc[h]
            q_dec = qk[h][:chunk] * jnp.exp2(b).astype(BF16)
            o_inter.append(_dot_nt(q_dec, state_t.astype(BF16)))
            b_last = b[chunk - 1:chunk, :]
            k_end = qk[h][chunk:] * jnp.exp2(b_last - b).astype(BF16)
            state_sc[h] = state_t * jnp.exp2(b_last) + _dot_tn(v[h], k_end)
        a = []
        for h in range(heads):
            ah = masks_ref[levels] * _dot_nt(qk[h][:chunk], qk[h][chunk:])
            for l in range(levels):
                dec = d0[h] if l == 0 else jnp.exp2(level_exponent(b_sc, h, l)).astype(BF16)
                prod = qk[h] * jnp.concatenate([dec, dec], axis=0)
                ah = ah + masks_ref[l] * _dot_nt(prod[:chunk], prod[chunk:])
            a.append(ah)
        return a, v, o_inter

    def finish(a, v, o_inter):
        return [_dot(a[h].astype(BF16), v[h]) + o_inter[h] for h in range(heads)]

    def emit(c, outs):
        rows = pl.ds(c * chunk, chunk)
        for h in range(heads):
            sl = slice(h * hd, (h + 1) * hd)
            gt = gate_ref[0, rows, sl]
            o_ref[0, rows, sl] = _rms(outs[h], gain_all[:, sl]) * (gt * jax.nn.sigmoid(gt))

    nchunks = tile // chunk
    prepared = {0: prepare(0)}
    mixed, outs = {}, {}
    for c in range(nchunks + 2):
        if c - 2 in outs:
            emit(c - 2, outs.pop(c - 2))
        if c + 1 < nchunks:
            prepared[c + 1] = prepare(c + 1)
        if c < nchunks:
            mixed[c] = mix(c, *prepared.pop(c))
        if c - 1 in mixed:
            outs[c - 1] = finish(*mixed.pop(c - 1))


def _hgrn(hg, lb_logits, out_gain, batch, seq, layer):
    width = hg.shape[-1] // 4
    heads = width // HGRN_HEAD_DIM
    chunk = HGRN_CHUNK
    tile = HGRN_SEQ_TILE
    tri, masks = _hgrn_tables(chunk)
    nrows = lb_logits.shape[0]

    def col(k):
        return pl.BlockSpec((1, tile, width), lambda b, t, k=k: (b, t, k))

    return pl.pallas_call(
        functools.partial(_hgrn_kernel, tile=tile, chunk=chunk, layer=layer, heads=heads),
        grid=(batch, seq // tile),
        in_specs=[
            col(0), col(1), col(2), col(3),
            pl.BlockSpec((nrows, width), lambda b, t: (0, 0)),
            pl.BlockSpec((1, width), lambda b, t: (0, 0)),
            pl.BlockSpec(tri.shape, lambda b, t: (0, 0)),
            pl.BlockSpec(masks.shape, lambda b, t: (0, 0, 0)),
        ],
        out_specs=pl.BlockSpec((1, tile, width), lambda b, t: (b, t, 0)),
        out_shape=jax.ShapeDtypeStruct((batch, seq, width), F32),
        scratch_shapes=(
            [pltpu.VMEM((heads, HGRN_HEAD_DIM, HGRN_HEAD_DIM), F32)]
            + 2 * [pltpu.VMEM((heads, chunk, HGRN_HEAD_DIM), F32)]),
        compiler_params=pltpu.CompilerParams(
            dimension_semantics=("arbitrary", "arbitrary"), vmem_limit_bytes=VMEM_LIMIT_BYTES),
        name="hgrn2",
    )(hg, hg, hg, hg, lb_logits, out_gain, jnp.asarray(tri, BF16), jnp.asarray(masks, F32))


def _out_ffn_kernel(x_ref, a_ref, r_ref, ga_ref, wo_ref, gpost_ref, gpre_ref, w1_ref, w2_ref,
                    gpost2_ref, o_ref, *, ff_tile):
    aw = a_ref.shape[1]
    d_ff = w1_ref.shape[1]
    part = x_ref.shape[0] // FFN_ROW_PARTS
    halves = [pl.ds(i * part, part) for i in range(FFN_ROW_PARTS)]
    x1 = []
    for r in halves:
        an = _rms(a_ref[r, :], ga_ref[...]).astype(BF16)
        mixed = _dot(an, wo_ref[:aw, :]) + _dot(r_ref[r, :].astype(BF16), wo_ref[aw:, :])
        x1.append(x_ref[r, :] + _rms(mixed, gpost_ref[...]))
    for r, x1h in zip(halves, x1):
        h = _rms(x1h, gpre_ref[...]).astype(BF16)
        acc = jnp.zeros(x1h.shape, F32)
        for j in range(d_ff // ff_tile):
            u = _dot(h, w1_ref[:, j * ff_tile:(j + 1) * ff_tile])
            u = jnp.square(jnp.maximum(u, 0.0)).astype(BF16)
            acc = acc + _dot(u, w2_ref[j * ff_tile:(j + 1) * ff_tile, :])
        o_ref[r, :] = x1h + _rms(acc, gpost2_ref[...])


def _out_ffn(x2, attn, rec, ga, wo_b, gpost, gpre, w1_b, w2_b, gpost2):
    n, d = x2.shape
    aw = attn.shape[1]
    rw = rec.shape[1]
    d_ff = w1_b.shape[1]

    def rows(w):
        return pl.BlockSpec((FFN_ROW_TILE, w), lambda i: (i, 0))

    def whole(shape):
        return pl.BlockSpec(shape, lambda i: (0, 0), pipeline_mode=pl.Buffered(1))

    return pl.pallas_call(
        functools.partial(_out_ffn_kernel, ff_tile=FFN_HIDDEN_TILE),
        grid=(n // FFN_ROW_TILE,),
        in_specs=[
            rows(d), rows(aw), rows(rw), whole((1, aw)), whole((aw + rw, d)), whole((1, d)),
            whole((1, d)), whole((d, d_ff)), whole((d_ff, d)), whole((1, d)),
        ],
        out_specs=rows(d),
        out_shape=jax.ShapeDtypeStruct((n, d), F32),
        compiler_params=pltpu.CompilerParams(
            dimension_semantics=("arbitrary",), vmem_limit_bytes=VMEM_LIMIT_BYTES),
        name="outproj_ffn",
    )(x2, attn, rec, ga, wo_b, gpost, gpre, w1_b, w2_b, gpost2)


def kernel(x, mix_pre_norm, w_in, attn_out_norm, hgrn_lb_logits, hgrn_out_norm, w_out,
           mix_post_norm, mlp_pre_norm, w_ff1, w_ff2, mlp_post_norm):
    batch, seq, d = x.shape
    depth = w_in.shape[0]
    aw = attn_out_norm.shape[1]
    n = batch * seq
    x2 = x.reshape(n, d)
    for layer in range(depth):
        qkv, hg = _inproj(x2, mix_pre_norm[layer][None], w_in[layer].astype(BF16), 3 * aw)
        attn = _attention(qkv.reshape(batch, seq, -1), batch, seq)
        rec = _hgrn(hg.reshape(batch, seq, -1), hgrn_lb_logits, hgrn_out_norm[layer][None],
                    batch, seq, layer)
        x2 = _out_ffn(x2, attn.reshape(n, -1), rec.reshape(n, -1), attn_out_norm[layer][None],
                      w_out[layer].astype(BF16), mix_post_norm[layer][None],
                      mlp_pre_norm[layer][None], w_ff1[layer].astype(BF16),
                      w_ff2[layer].astype(BF16), mlp_post_norm[layer][None])
    return x2.reshape(batch, seq, d)
```

```python
import functools

import numpy as np
import jax
import jax.numpy as jnp
from jax import lax
from jax.experimental import pallas as pl
from jax.experimental.pallas import tpu as pltpu

F32 = jnp.float32
BF16 = jnp.bfloat16

RMS_EPS = 1e-6
ATTN_HEAD_DIM = 64
ATTN_BLOCK = 128
ATTN_GROUP = 8
DILATED_PATTERNS = ((128, 1), (512, 4), (2048, 16))
HGRN_HEAD_DIM = 128
HGRN_CHUNK = 64
HGRN_SEQ_TILE = 2048
LANES = 128
SUBLANES = 8
MASKED_SCORE = -1e30
F32_MIN_NORMAL = 1.1754944e-38
LOG2E = 1.4426950408889634
VMEM_LIMIT_BYTES = 56 * 1024 * 1024
INPROJ_ROW_TILE = 1024
INPROJ_ROW_PARTS = 4
FFN_ROW_TILE = 1024
FFN_HIDDEN_TILE = 1024
FFN_ROW_PARTS = 4


def _rms(x, gain):
    return x * lax.rsqrt(jnp.mean(x * x, axis=-1, keepdims=True) + RMS_EPS) * gain


def _dot(a, b):
    return jnp.dot(a, b, preferred_element_type=F32)


def _dot_nt(a, b):
    return lax.dot_general(a, b, (((1,), (1,)), ((), ())), preferred_element_type=F32)


def _dot_tn(a, b):
    return lax.dot_general(a, b, (((0,), (0,)), ((), ())), preferred_element_type=F32)


def _inproj_kernel(x_ref, g_ref, w_ref, qkv_ref, hg_ref, *, attn_cols):
    part = x_ref.shape[0] // INPROJ_ROW_PARTS
    parts = [pl.ds(i * part, part) for i in range(INPROJ_ROW_PARTS)]
    normed = [_rms(x_ref[r, :], g_ref[...]).astype(BF16) for r in parts]
    for r, h in zip(parts, normed):
        qkv_ref[r, :] = _dot(h, w_ref[:, :attn_cols])
        hg_ref[r, :] = _dot(h, w_ref[:, attn_cols:])


def _inproj(x2, gain, w_in_b, attn_cols):
    n, d = x2.shape
    width = w_in_b.shape[1]
    return pl.pallas_call(
        functools.partial(_inproj_kernel, attn_cols=attn_cols),
        grid=(n // INPROJ_ROW_TILE,),
        in_specs=[
            pl.BlockSpec((INPROJ_ROW_TILE, d), lambda i: (i, 0)),
            pl.BlockSpec((1, d), lambda i: (0, 0)),
            pl.BlockSpec((d, width), lambda i: (0, 0), pipeline_mode=pl.Buffered(1)),
        ],
        out_specs=[
            pl.BlockSpec((INPROJ_ROW_TILE, attn_cols), lambda i: (i, 0)),
            pl.BlockSpec((INPROJ_ROW_TILE, width - attn_cols), lambda i: (i, 0)),
        ],
        out_shape=[
            jax.ShapeDtypeStruct((n, attn_cols), F32),
            jax.ShapeDtypeStruct((n, width - attn_cols), F32),
        ],
        compiler_params=pltpu.CompilerParams(
            dimension_semantics=("arbitrary",), vmem_limit_bytes=VMEM_LIMIT_BYTES),
        name="inproj",
    )(x2, gain, w_in_b)


def _attn_kernel(q_ref, k_ref, v_ref, o_ref, m_sc, acc_sc, den_sc, bias_sc, *, seq):
    hp = pl.program_id(1)
    blk = ATTN_BLOCK
    patterns = sorted(DILATED_PATTERNS, key=lambda p: -p[1])
    npat = len(patterns)
    lane = lax.broadcasted_iota(jnp.int32, (blk, LANES), 1)
    head0 = lane < ATTN_HEAD_DIM
    ones_tile = jnp.ones((blk, LANES), BF16)

    row = lax.broadcasted_iota(jnp.int32, (2 * blk, 2 * blk), 0)
    col = lax.broadcasted_iota(jnp.int32, (2 * blk, 2 * blk), 1)
    qi = row & (blk - 1)
    kj = col & (blk - 1)
    is_cur = col >= blk
    dist = jnp.where(is_cur, qi - kj, qi - kj + blk)
    valid = (dist >= 0) & (dist <= blk)
    slope = jnp.exp2((-(2 * hp + (row >> 7) + 1)).astype(F32))
    for bi, (_, dil) in enumerate(patterns):
        bias = -(slope * LOG2E) * (dist * dil).astype(F32)
        bias_sc[bi] = jnp.where(valid, bias, MASKED_SCORE).astype(BF16)

    def tile_rows_of(dil, run):
        start, first, count = run
        step = blk * dil

        def rows(st):
            return pl.ds(st, blk, stride=dil) if dil > 1 else pl.ds(st, blk)

        return [None if first else rows(start - step)] + [
            rows(start + g * step) for g in range(count)]

    def group(bi, dil, runs):
        blocks = []
        for run in runs:
            tile_rows = tile_rows_of(dil, run)
            keys = [None if r is None else k_ref[0, r, :].astype(BF16) for r in tile_rows]
            vals = [None if r is None else
                    jnp.concatenate([v_ref[0, r, :].astype(BF16), ones_tile], axis=1)
                    for r in tile_rows]
            for g in range(run[2]):
                blocks.append((tile_rows[g + 1], keys[g], keys[g + 1], vals[g], vals[g + 1]))
        scores = []
        for r, kp, kc, _, _ in blocks:
            qb = q_ref[0, r, :] * (ATTN_HEAD_DIM ** -0.5 * LOG2E)
            qs = jnp.concatenate([jnp.where(head0, qb, 0.0), jnp.where(head0, 0.0, qb)],
                                 axis=0).astype(BF16)
            if kp is None:
                scores.append(_dot_nt(qs, kc).astype(BF16) + bias_sc[bi, :, blk:])
            else:
                kcat = jnp.concatenate([kp, kc], axis=0)
                scores.append(_dot_nt(qs, kcat).astype(BF16) + bias_sc[bi])
        maxes = [jnp.max(s, axis=-1, keepdims=True) for s in scores]
        probs = [jnp.exp2(s - mx) for s, mx in zip(scores, maxes)]
        outs = []
        for (_, _, _, vp, vc), p in zip(blocks, probs):
            outs.append(_dot(p, vc) if vp is None else _dot(p, jnp.concatenate([vp, vc], axis=0)))
        for (r, _, _, _, _), out, mx in zip(blocks, outs, maxes):
            mx = mx.astype(F32)
            acc = jnp.where(head0, out[:blk, :LANES], out[blk:, :LANES])
            den = jnp.where(head0, out[:blk, LANES:], out[blk:, LANES:])
            mb = jnp.where(head0, jnp.broadcast_to(mx[:blk], (blk, LANES)),
                           jnp.broadcast_to(mx[blk:], (blk, LANES)))
            if bi > 0:
                m_old = m_sc[r, :]
                m_new = jnp.maximum(m_old, mb)
                w_old = jnp.exp2(m_old - m_new)
                w_new = jnp.exp2(mb - m_new)
                acc = acc_sc[r, :] * w_old + acc * w_new
                den = den_sc[r, :] * w_old + den * w_new
                mb = m_new
            if bi < npat - 1:
                m_sc[r, :] = mb
                acc_sc[r, :] = acc
                den_sc[r, :] = den
            else:
                o_ref[0, r, :] = acc / den

    for bi, (_, dil) in enumerate(patterns):
        run_len = seq // dil // blk
        for g in range(seq // blk // ATTN_GROUP):
            if run_len >= ATTN_GROUP:
                per_residue = run_len // ATTN_GROUP
                i = g % per_residue
                runs = [(g // per_residue + i * (ATTN_GROUP * blk * dil), i == 0, ATTN_GROUP)]
            else:
                nruns = ATTN_GROUP // run_len
                runs = [(g * nruns + u, True, run_len) for u in range(nruns)]
            group(bi, dil, runs)


def _attention(qkv, batch, seq):
    width = qkv.shape[-1] // 3
    pairs = width // LANES
    blk = ATTN_BLOCK
    return pl.pallas_call(
        functools.partial(_attn_kernel, seq=seq),
        grid=(batch, pairs),
        in_specs=[
            pl.BlockSpec((1, seq, LANES), lambda b, h: (b, 0, h)),
            pl.BlockSpec((1, seq, LANES), lambda b, h: (b, 0, pairs + h)),
            pl.BlockSpec((1, seq, LANES), lambda b, h: (b, 0, 2 * pairs + h)),
        ],
        out_specs=pl.BlockSpec((1, seq, LANES), lambda b, h: (b, 0, h)),
        out_shape=jax.ShapeDtypeStruct((batch, seq, width), F32),
        scratch_shapes=[
            pltpu.VMEM((seq, LANES), F32),
            pltpu.VMEM((seq, LANES), F32),
            pltpu.VMEM((seq, LANES), F32),
            pltpu.VMEM((len(DILATED_PATTERNS), 2 * blk, 2 * blk), BF16),
        ],
        compiler_params=pltpu.CompilerParams(
            dimension_semantics=("arbitrary", "arbitrary"), vmem_limit_bytes=VMEM_LIMIT_BYTES),
        name="dilated_attention",
    )(qkv, qkv, qkv)


def _hgrn_tables(chunk):
    levels = int(np.log2(chunk))
    t = np.arange(chunk)
    masks = np.zeros((levels + 1, chunk, chunk), np.float32)
    for l in range(levels):
        blk = t // (1 << l)
        masks[l] = ((blk[:, None] & 1) == 1) & (blk[None, :] == blk[:, None] - 1)
    masks[levels] = np.eye(chunk, dtype=np.float32)
    assert np.array_equal(masks.sum(0), np.tril(np.ones((chunk, chunk), np.float32)))
    return np.tril(np.ones((chunk, chunk), np.float32)), masks


def _hgrn_kernel(q_ref, f_ref, i_ref, gate_ref, lbl_ref, gain_ref, tri_ref, masks_ref,
                 o_ref, state_sc, *b_slots, tile, chunk, layer, heads):
    hd = HGRN_HEAD_DIM
    levels = masks_ref.shape[0] - 1

    @pl.when(pl.program_id(1) == 0)
    def _():
        state_sc[...] = jnp.zeros(state_sc.shape, F32)

    logits = lbl_ref[...]
    e = jnp.exp(logits - jnp.max(logits, axis=0, keepdims=True))
    lb_all = jnp.sum(e[0:layer + 1, :], axis=0, keepdims=True) / jnp.sum(e, axis=0, keepdims=True)
    gain_all = gain_ref[...]
    row = lax.broadcasted_iota(jnp.int32, (chunk, hd), 0)
    odd_row = (row & 1) == 1
    low_half = (row & 4) == 0
    tri = tri_ref[...]

    role_sign = {l: jnp.where(((row >> l) & 1) == 1, 1.0, -1.0)
                 for l in range(1, levels) if (1 << l) < SUBLANES}

    def level_exponent(b_sc, h, l):
        m = 1 << l

        def bcast(r):
            return jnp.broadcast_to(b_sc[h, r:r + 1, :], (SUBLANES, hd))
        pieces = []
        for s in range(chunk // SUBLANES):
            first = s * SUBLANES
            slab = b_sc[h, first:first + SUBLANES, :]
            if m >= SUBLANES:
                bd = bcast(first // (2 * m) * (2 * m) + m - 1)
                pieces.append(slab - bd if (first // m) % 2 == 1 else bd - slab)
            else:
                if m == 4:
                    bd = bcast(first + 3)
                else:
                    bd = jnp.where(low_half[:SUBLANES], bcast(first + 1), bcast(first + 5))
                pieces.append((slab - bd) * role_sign[l][:SUBLANES])
        return jnp.concatenate(pieces, axis=0)

    def prepare(c):
        b_sc = b_slots[c % 2]
        rows = pl.ds(c * chunk, chunk)
        qk, d0, v = [], [], []
        for h in range(heads):
            sl = slice(h * hd, (h + 1) * hd)
            lb = lb_all[:, sl]
            qr = q_ref[0, rows, sl]
            open_part = (1.0 - lb) * jax.nn.sigmoid(f_ref[0, rows, sl])
            f = lb + open_part
            qk.append(jnp.concatenate([qr * jax.nn.sigmoid(qr), (1.0 - lb) - open_part],
                                      axis=0).astype(BF16))
            d0.append(jnp.where(odd_row, f, 1.0).astype(BF16))
            v.append(i_ref[0, rows, sl].astype(BF16))
            g = jnp.log2(jnp.maximum(f, F32_MIN_NORMAL))
            g_hi = g.astype(BF16)
            g_lo = (g - g_hi.astype(F32)).astype(BF16)
            b_sc[h] = _dot(tri, g_hi) + _dot(tri, g_lo)
        return qk, d0, v

    def mix(c, qk, d0, v):
        b_sc = b_slots[c % 2]
        o_inter = []
        for h in range(heads):
            b = b_sc[h]
            state_t = state_sc[h]
            q_dec = qk[h][:chunk] * jnp.exp2(b).astype(BF16)
            o_inter.append(_dot_nt(q_dec, state_t.astype(BF16)))
            b_last = b[chunk - 1:chunk, :]
            k_end = qk[h][chunk:] * jnp.exp2(b_last - b).astype(BF16)
            state_sc[h] = state_t * jnp.exp2(b_last) + _dot_tn(v[h], k_end)
        a = []
        for h in range(heads):
            ah = masks_ref[levels] * _dot_nt(qk[h][:chunk], qk[h][chunk:])
            for l in range(levels):
                dec = d0[h] if l == 0 else jnp.exp2(level_exponent(b_sc, h, l)).astype(BF16)
                prod = qk[h] * jnp.concatenate([dec, dec], axis=0)
                ah = ah + masks_ref[l] * _dot_nt(prod[:chunk], prod[chunk:])
            a.append(ah)
        return a, v, o_inter

    def finish(a, v, o_inter):
        return [_dot(a[h].astype(BF16), v[h]) + o_inter[h] for h in range(heads)]

    def emit(c, outs):
        rows = pl.ds(c * chunk, chunk)
        for h in range(heads):
            sl = slice(h * hd, (h + 1) * hd)
            gt = gate_ref[0, rows, sl]
            o_ref[0, rows, sl] = _rms(outs[h], gain_all[:, sl]) * (gt * jax.nn.sigmoid(gt))

    nchunks = tile // chunk
    prepared = {0: prepare(0)}
    mixed, outs = {}, {}
    for c in range(nchunks + 2):
        if c - 2 in outs:
            emit(c - 2, outs.pop(c - 2))
        if c + 1 < nchunks:
            prepared[c + 1] = prepare(c + 1)
        if c < nchunks:
            mixed[c] = mix(c, *prepared.pop(c))
        if c - 1 in mixed:
            outs[c - 1] = finish(*mixed.pop(c - 1))


def _hgrn(hg, lb_logits, out_gain, batch, seq, layer):
    width = hg.shape[-1] // 4
    heads = width // HGRN_HEAD_DIM
    chunk = HGRN_CHUNK
    tile = HGRN_SEQ_TILE
    tri, masks = _hgrn_tables(chunk)
    nrows = lb_logits.shape[0]

    def col(k):
        return pl.BlockSpec((1, tile, width), lambda b, t, k=k: (b, t, k))

    return pl.pallas_call(
        functools.partial(_hgrn_kernel, tile=tile, chunk=chunk, layer=layer, heads=heads),
        grid=(batch, seq // tile),
        in_specs=[
            col(0), col(1), col(2), col(3),
            pl.BlockSpec((nrows, width), lambda b, t: (0, 0)),
            pl.BlockSpec((1, width), lambda b, t: (0, 0)),
            pl.BlockSpec(tri.shape, lambda b, t: (0, 0)),
            pl.BlockSpec(masks.shape, lambda b, t: (0, 0, 0)),
        ],
        out_specs=pl.BlockSpec((1, tile, width), lambda b, t: (b, t, 0)),
        out_shape=jax.ShapeDtypeStruct((batch, seq, width), F32),
        scratch_shapes=(
            [pltpu.VMEM((heads, HGRN_HEAD_DIM, HGRN_HEAD_DIM), F32)]
            + 2 * [pltpu.VMEM((heads, chunk, HGRN_HEAD_DIM), F32)]),
        compiler_params=pltpu.CompilerParams(
            dimension_semantics=("arbitrary", "arbitrary"), vmem_limit_bytes=VMEM_LIMIT_BYTES),
        name="hgrn2",
    )(hg, hg, hg, hg, lb_logits, out_gain, jnp.asarray(tri, BF16), jnp.asarray(masks, F32))


def _out_ffn_kernel(x_ref, a_ref, r_ref, ga_ref, wo_ref, gpost_ref, gpre_ref, w1_ref, w2_ref,
                    gpost2_ref, o_ref, *, ff_tile):
    aw = a_ref.shape[1]
    d_ff = w1_ref.shape[1]
    part = x_ref.shape[0] // FFN_ROW_PARTS
    halves = [pl.ds(i * part, part) for i in range(FFN_ROW_PARTS)]
    x1 = []
    for r in halves:
        an = _rms(a_ref[r, :], ga_ref[...]).astype(BF16)
        mixed = _dot(an, wo_ref[:aw, :]) + _dot(r_ref[r, :].astype(BF16), wo_ref[aw:, :])
        x1.append(x_ref[r, :] + _rms(mixed, gpost_ref[...]))
    for r, x1h in zip(halves, x1):
        h = _rms(x1h, gpre_ref[...]).astype(BF16)
        acc = jnp.zeros(x1h.shape, F32)
        for j in range(d_ff // ff_tile):
            u = _dot(h, w1_ref[:, j * ff_tile:(j + 1) * ff_tile])
            u = jnp.square(jnp.maximum(u, 0.0)).astype(BF16)
            acc = acc + _dot(u, w2_ref[j * ff_tile:(j + 1) * ff_tile, :])
        o_ref[r, :] = x1h + _rms(acc, gpost2_ref[...])


def _out_ffn(x2, attn, rec, ga, wo_b, gpost, gpre, w1_b, w2_b, gpost2):
    n, d = x2.shape
    aw = attn.shape[1]
    rw = rec.shape[1]
    d_ff = w1_b.shape[1]

    def rows(w):
        return pl.BlockSpec((FFN_ROW_TILE, w), lambda i: (i, 0))

    def whole(shape):
        return pl.BlockSpec(shape, lambda i: (0, 0), pipeline_mode=pl.Buffered(1))

    return pl.pallas_call(
        functools.partial(_out_ffn_kernel, ff_tile=FFN_HIDDEN_TILE),
        grid=(n // FFN_ROW_TILE,),
        in_specs=[
            rows(d), rows(aw), rows(rw), whole((1, aw)), whole((aw + rw, d)), whole((1, d)),
            whole((1, d)), whole((d, d_ff)), whole((d_ff, d)), whole((1, d)),
        ],
        out_specs=rows(d),
        out_shape=jax.ShapeDtypeStruct((n, d), F32),
        compiler_params=pltpu.CompilerParams(
            dimension_semantics=("arbitrary",), vmem_limit_bytes=VMEM_LIMIT_BYTES),
        name="outproj_ffn",
    )(x2, attn, rec, ga, wo_b, gpost, gpre, w1_b, w2_b, gpost2)


def kernel(x, mix_pre_norm, w_in, attn_out_norm, hgrn_lb_logits, hgrn_out_norm, w_out,
           mix_post_norm, mlp_pre_norm, w_ff1, w_ff2, mlp_post_norm):
    batch, seq, d = x.shape
    depth = w_in.shape[0]
    aw = attn_out_norm.shape[1]
    n = batch * seq
    x2 = x.reshape(n, d)
    for layer in range(depth):
        qkv, hg = _inproj(x2, mix_pre_norm[layer][None], w_in[layer].astype(BF16), 3 * aw)
        attn = _attention(qkv.reshape(batch, seq, -1), batch, seq)
        rec = _hgrn(hg.reshape(batch, seq, -1), hgrn_lb_logits, hgrn_out_norm[layer][None],
                    batch, seq, layer)
        x2 = _out_ffn(x2, attn.reshape(n, -1), rec.reshape(n, -1), attn_out_norm[layer][None],
                      w_out[layer].astype(BF16), mix_post_norm[layer][None],
                      mlp_pre_norm[layer][None], w_ff1[layer].astype(BF16),
                      w_ff2[layer].astype(BF16), mlp_post_norm[layer][None])
    return x2.reshape(batch, seq, d)
```

```python
import functools

import numpy as np
import jax
import jax.numpy as jnp
from jax import lax
from jax.experimental import pallas as pl
from jax.experimental.pallas import tpu as pltpu

F32 = jnp.float32
BF16 = jnp.bfloat16

RMS_EPS = 1e-6
ATTN_HEAD_DIM = 64
ATTN_BLOCK = 128
ATTN_GROUP = 8
DILATED_PATTERNS = ((128, 1), (512, 4), (2048, 16))
HGRN_HEAD_DIM = 128
HGRN_CHUNK = 64
HGRN_SEQ_TILE = 2048
LANES = 128
SUBLANES = 8
BF16_SUBLANES = 16
MASKED_SCORE = -1e30
F32_MIN_NORMAL = 1.1754944e-38
LOG2E = 1.4426950408889634
VMEM_LIMIT_BYTES = 56 * 1024 * 1024
INPROJ_ROW_TILE = 1024
INPROJ_ROW_PARTS = 4
FFN_ROW_TILE = 1024
FFN_HIDDEN_TILE = 1024
FFN_ROW_PARTS = 4


def _rms(x, gain):
    return x * lax.rsqrt(jnp.mean(x * x, axis=-1, keepdims=True) + RMS_EPS) * gain


def _dot(a, b):
    return jnp.dot(a, b, preferred_element_type=F32)


def _dot_nt(a, b):
    return lax.dot_general(a, b, (((1,), (1,)), ((), ())), preferred_element_type=F32)


def _dot_tn(a, b):
    return lax.dot_general(a, b, (((0,), (0,)), ((), ())), preferred_element_type=F32)


def _inproj_kernel(x_ref, g_ref, w_ref, *rest, attn_cols):
    nlater = (len(rest) - 2) // 2
    later_f32, (qkv_ref, hg_ref), later_b16 = rest[:nlater], rest[nlater:nlater + 2], rest[nlater + 2:]
    for src, dst in zip(later_f32, later_b16):
        dst[...] = src[...].astype(BF16)
    part = x_ref.shape[0] // INPROJ_ROW_PARTS
    parts = [pl.ds(i * part, part) for i in range(INPROJ_ROW_PARTS)]
    normed = [_rms(x_ref[r, :], g_ref[...]).astype(BF16) for r in parts]
    for r, h in zip(parts, normed):
        qkv_ref[r, :] = _dot(h, w_ref[:, :attn_cols])
        hg_ref[r, :] = _dot(h, w_ref[:, attn_cols:])


def _inproj(x2, gain, w_in_b, attn_cols, later_weights):
    n, d = x2.shape
    width = w_in_b.shape[1]
    steps = n // INPROJ_ROW_TILE
    slices = []
    for w in later_weights:
        rows = w.shape[0] // steps
        assert rows * steps == w.shape[0] and rows % BF16_SUBLANES == 0
        slices.append(pl.BlockSpec((rows, w.shape[1]), lambda i: (i, 0)))
    return pl.pallas_call(
        functools.partial(_inproj_kernel, attn_cols=attn_cols),
        grid=(steps,),
        in_specs=[
            pl.BlockSpec((INPROJ_ROW_TILE, d), lambda i: (i, 0)),
            pl.BlockSpec((1, d), lambda i: (0, 0)),
            pl.BlockSpec((d, width), lambda i: (0, 0), pipeline_mode=pl.Buffered(1)),
        ] + slices,
        out_specs=[
            pl.BlockSpec((INPROJ_ROW_TILE, attn_cols), lambda i: (i, 0)),
            pl.BlockSpec((INPROJ_ROW_TILE, width - attn_cols), lambda i: (i, 0)),
        ] + slices,
        out_shape=[
            jax.ShapeDtypeStruct((n, attn_cols), F32),
            jax.ShapeDtypeStruct((n, width - attn_cols), F32),
        ] + [jax.ShapeDtypeStruct(w.shape, BF16) for w in later_weights],
        compiler_params=pltpu.CompilerParams(
            dimension_semantics=("arbitrary",), vmem_limit_bytes=VMEM_LIMIT_BYTES),
        name="inproj",
    )(x2, gain, w_in_b, *later_weights)


def _attn_kernel(q_ref, k_ref, v_ref, o_ref, m_sc, acc_sc, den_sc, bias_sc, *, seq):
    hp = pl.program_id(1)
    blk = ATTN_BLOCK
    patterns = sorted(DILATED_PATTERNS, key=lambda p: -p[1])
    npat = len(patterns)
    lane = lax.broadcasted_iota(jnp.int32, (blk, LANES), 1)
    head0 = lane < ATTN_HEAD_DIM
    ones_tile = jnp.ones((blk, LANES), BF16)

    row = lax.broadcasted_iota(jnp.int32, (2 * blk, 2 * blk), 0)
    col = lax.broadcasted_iota(jnp.int32, (2 * blk, 2 * blk), 1)
    qi = row & (blk - 1)
    kj = col & (blk - 1)
    is_cur = col >= blk
    dist = jnp.where(is_cur, qi - kj, qi - kj + blk)
    valid = (dist >= 0) & (dist <= blk)
    slope = jnp.exp2((-(2 * hp + (row >> 7) + 1)).astype(F32))
    for bi, (_, dil) in enumerate(patterns):
        bias = -(slope * LOG2E) * (dist * dil).astype(F32)
        bias_sc[bi] = jnp.where(valid, bias, MASKED_SCORE).astype(BF16)

    def tile_rows_of(dil, run):
        start, first, count = run
        step = blk * dil

        def rows(st):
            return pl.ds(st, blk, stride=dil) if dil > 1 else pl.ds(st, blk)

        return [None if first else rows(start - step)] + [
            rows(start + g * step) for g in range(count)]

    def group(bi, dil, runs):
        blocks = []
        for run in runs:
            tile_rows = tile_rows_of(dil, run)
            keys = [None if r is None else k_ref[0, r, :].astype(BF16) for r in tile_rows]
            vals = [None if r is None else
                    jnp.concatenate([v_ref[0, r, :].astype(BF16), ones_tile], axis=1)
                    for r in tile_rows]
            for g in range(run[2]):
                blocks.append((tile_rows[g + 1], keys[g], keys[g + 1], vals[g], vals[g + 1]))
        scores = []
        for r, kp, kc, _, _ in blocks:
            qb = q_ref[0, r, :] * (ATTN_HEAD_DIM ** -0.5 * LOG2E)
            qs = jnp.concatenate([jnp.where(head0, qb, 0.0), jnp.where(head0, 0.0, qb)],
                                 axis=0).astype(BF16)
            if kp is None:
                scores.append(_dot_nt(qs, kc).astype(BF16) + bias_sc[bi, :, blk:])
            else:
                kcat = jnp.concatenate([kp, kc], axis=0)
                scores.append(_dot_nt(qs, kcat).astype(BF16) + bias_sc[bi])
        maxes = [jnp.max(s, axis=-1, keepdims=True) for s in scores]
        probs = [jnp.exp2(s - mx) for s, mx in zip(scores, maxes)]
        outs = []
        for (_, _, _, vp, vc), p in zip(blocks, probs):
            outs.append(_dot(p, vc) if vp is None else _dot(p, jnp.concatenate([vp, vc], axis=0)))
        for (r, _, _, _, _), out, mx in zip(blocks, outs, maxes):
            mx = mx.astype(F32)
            acc = jnp.where(head0, out[:blk, :LANES], out[blk:, :LANES])
            den = jnp.where(head0, out[:blk, LANES:], out[blk:, LANES:])
            mb = jnp.where(head0, jnp.broadcast_to(mx[:blk], (blk, LANES)),
                           jnp.broadcast_to(mx[blk:], (blk, LANES)))
            if bi > 0:
                m_old = m_sc[r, :]
                m_new = jnp.maximum(m_old, mb)
                w_old = jnp.exp2(m_old - m_new)
                w_new = jnp.exp2(mb - m_new)
                acc = acc_sc[r, :] * w_old + acc * w_new
                den = den_sc[r, :] * w_old + den * w_new
                mb = m_new
            if bi < npat - 1:
                m_sc[r, :] = mb
                acc_sc[r, :] = acc
                den_sc[r, :] = den
            else:
                o_ref[0, r, :] = acc / den

    for bi, (_, dil) in enumerate(patterns):
        run_len = seq // dil // blk
        for g in range(seq // blk // ATTN_GROUP):
            if run_len >= ATTN_GROUP:
                per_residue = run_len // ATTN_GROUP
                i = g % per_residue
                runs = [(g // per_residue + i * (ATTN_GROUP * blk * dil), i == 0, ATTN_GROUP)]
            else:
                nruns = ATTN_GROUP // run_len
                runs = [(g * nruns + u, True, run_len) for u in range(nruns)]
            group(bi, dil, runs)


def _attention(qkv, batch, seq):
    width = qkv.shape[-1] // 3
    pairs = width // LANES
    blk = ATTN_BLOCK
    return pl.pallas_call(
        functools.partial(_attn_kernel, seq=seq),
        grid=(batch, pairs),
        in_specs=[
            pl.BlockSpec((1, seq, LANES), lambda b, h: (b, 0, h)),
            pl.BlockSpec((1, seq, LANES), lambda b, h: (b, 0, pairs + h)),
            pl.BlockSpec((1, seq, LANES), lambda b, h: (b, 0, 2 * pairs + h)),
        ],
        out_specs=pl.BlockSpec((1, seq, LANES), lambda b, h: (b, 0, h)),
        out_shape=jax.ShapeDtypeStruct((batch, seq, width), F32),
        scratch_shapes=[
            pltpu.VMEM((seq, LANES), F32),
            pltpu.VMEM((seq, LANES), F32),
            pltpu.VMEM((seq, LANES), F32),
            pltpu.VMEM((len(DILATED_PATTERNS), 2 * blk, 2 * blk), BF16),
        ],
        compiler_params=pltpu.CompilerParams(
            dimension_semantics=("arbitrary", "arbitrary"), vmem_limit_bytes=VMEM_LIMIT_BYTES),
        name="dilated_attention",
    )(qkv, qkv, qkv)


def _hgrn_tables(chunk):
    levels = int(np.log2(chunk))
    t = np.arange(chunk)
    masks = np.zeros((levels + 1, chunk, chunk), np.float32)
    for l in range(levels):
        blk = t // (1 << l)
        masks[l] = ((blk[:, None] & 1) == 1) & (blk[None, :] == blk[:, None] - 1)
    masks[levels] = np.eye(chunk, dtype=np.float32)
    assert np.array_equal(masks.sum(0), np.tril(np.ones((chunk, chunk), np.float32)))
    return np.tril(np.ones((chunk, chunk), np.float32)), masks


def _hgrn_kernel(q_ref, f_ref, i_ref, gate_ref, lbl_ref, gain_ref, tri_ref, masks_ref,
                 o_ref, state_sc, *b_slots, tile, chunk, layer, heads):
    hd = HGRN_HEAD_DIM
    levels = masks_ref.shape[0] - 1

    @pl.when(pl.program_id(1) == 0)
    def _():
        state_sc[...] = jnp.zeros(state_sc.shape, F32)

    logits = lbl_ref[...]
    e = jnp.exp(logits - jnp.max(logits, axis=0, keepdims=True))
    lb_all = jnp.sum(e[0:layer + 1, :], axis=0, keepdims=True) / jnp.sum(e, axis=0, keepdims=True)
    gain_all = gain_ref[...]
    row = lax.broadcasted_iota(jnp.int32, (chunk, hd), 0)
    odd_row = (row & 1) == 1
    low_half = (row & 4) == 0
    tri = tri_ref[...]

    role_sign = {l: jnp.where(((row >> l) & 1) == 1, 1.0, -1.0)
                 for l in range(1, levels) if (1 << l) < SUBLANES}

    def level_exponent(b_sc, h, l):
        m = 1 << l

        def bcast(r):
            return jnp.broadcast_to(b_sc[h, r:r + 1, :], (SUBLANES, hd))
        pieces = []
        for s in range(chunk // SUBLANES):
            first = s * SUBLANES
            slab = b_sc[h, first:first + SUBLANES, :]
            if m >= SUBLANES:
                bd = bcast(first // (2 * m) * (2 * m) + m - 1)
                pieces.append(slab - bd if (first // m) % 2 == 1 else bd - slab)
            else:
                if m == 4:
                    bd = bcast(first + 3)
                else:
                    bd = jnp.where(low_half[:SUBLANES], bcast(first + 1), bcast(first + 5))
                pieces.append((slab - bd) * role_sign[l][:SUBLANES])
        return jnp.concatenate(pieces, axis=0)

    def prepare(c):
        b_sc = b_slots[c % 2]
        rows = pl.ds(c * chunk, chunk)
        qk, d0, v = [], [], []
        for h in range(heads):
            sl = slice(h * hd, (h + 1) * hd)
            lb = lb_all[:, sl]
            qr = q_ref[0, rows, sl]
            open_part = (1.0 - lb) * jax.nn.sigmoid(f_ref[0, rows, sl])
            f = lb + open_part
            qk.append(jnp.concatenate([qr * jax.nn.sigmoid(qr), (1.0 - lb) - open_part],
                                      axis=0).astype(BF16))
            d0.append(jnp.where(odd_row, f, 1.0).astype(BF16))
            v.append(i_ref[0, rows, sl].astype(BF16))
            g = jnp.log2(jnp.maximum(f, F32_MIN_NORMAL))
            g_hi = g.astype(BF16)
            g_lo = (g - g_hi.astype(F32)).astype(BF16)
            b_sc[h] = _dot(tri, g_hi) + _dot(tri, g_lo)
        return qk, d0, v

    def mix(c, qk, d0, v):
        b_sc = b_slots[c % 2]
        o_inter = []
        for h in range(heads):
            b = b_sc[h]
            state_t = state_sc[h]
            q_dec = qk[h][:chunk] * jnp.exp2(b).astype(BF16)
            o_inter.append(_dot_nt(q_dec, state_t.astype(BF16)))
            b_last = b[chunk - 1:chunk, :]
            k_end = qk[h][chunk:] * jnp.exp2(b_last - b).astype(BF16)
            state_sc[h] = state_t * jnp.exp2(b_last) + _dot_tn(v[h], k_end)
        a = []
        for h in range(heads):
            ah = masks_ref[levels] * _dot_nt(qk[h][:chunk], qk[h][chunk:])
            for l in range(levels):
                dec = d0[h] if l == 0 else jnp.exp2(level_exponent(b_sc, h, l)).astype(BF16)
                prod = qk[h] * jnp.concatenate([dec, dec], axis=0)
                ah = ah + masks_ref[l] * _dot_nt(prod[:chunk], prod[chunk:])
            a.append(ah)
        return a, v, o_inter

    def finish(a, v, o_inter):
        return [_dot(a[h].astype(BF16), v[h]) + o_inter[h] for h in range(heads)]

    def emit(c, outs):
        rows = pl.ds(c * chunk, chunk)
        for h in range(heads):
            sl = slice(h * hd, (h + 1) * hd)
            gt = gate_ref[0, rows, sl]
            o_ref[0, rows, sl] = _rms(outs[h], gain_all[:, sl]) * (gt * jax.nn.sigmoid(gt))

    nchunks = tile // chunk
    prepared = {0: prepare(0)}
    mixed, outs = {}, {}
    for c in range(nchunks + 2):
        if c - 2 in outs:
            emit(c - 2, outs.pop(c - 2))
        if c + 1 < nchunks:
            prepared[c + 1] = prepare(c + 1)
        if c < nchunks:
            mixed[c] = mix(c, *prepared.pop(c))
        if c - 1 in mixed:
            outs[c - 1] = finish(*mixed.pop(c - 1))


def _hgrn(hg, lb_logits, out_gain, batch, seq, layer):
    width = hg.shape[-1] // 4
    heads = width // HGRN_HEAD_DIM
    chunk = HGRN_CHUNK
    tile = HGRN_SEQ_TILE
    tri, masks = _hgrn_tables(chunk)
    nrows = lb_logits.shape[0]

    def col(k):
        return pl.BlockSpec((1, tile, width), lambda b, t, k=k: (b, t, k))

    return pl.pallas_call(
        functools.partial(_hgrn_kernel, tile=tile, chunk=chunk, layer=layer, heads=heads),
        grid=(batch, seq // tile),
        in_specs=[
            col(0), col(1), col(2), col(3),
            pl.BlockSpec((nrows, width), lambda b, t: (0, 0)),
            pl.BlockSpec((1, width), lambda b, t: (0, 0)),
            pl.BlockSpec(tri.shape, lambda b, t: (0, 0)),
            pl.BlockSpec(masks.shape, lambda b, t: (0, 0, 0)),
        ],
        out_specs=pl.BlockSpec((1, tile, width), lambda b, t: (b, t, 0)),
        out_shape=jax.ShapeDtypeStruct((batch, seq, width), F32),
        scratch_shapes=(
            [pltpu.VMEM((heads, HGRN_HEAD_DIM, HGRN_HEAD_DIM), F32)]
            + 2 * [pltpu.VMEM((heads, chunk, HGRN_HEAD_DIM), F32)]),
        compiler_params=pltpu.CompilerParams(
            dimension_semantics=("arbitrary", "arbitrary"), vmem_limit_bytes=VMEM_LIMIT_BYTES),
        name="hgrn2",
    )(hg, hg, hg, hg, lb_logits, out_gain, jnp.asarray(tri, BF16), jnp.asarray(masks, F32))


def _out_ffn_kernel(x_ref, a_ref, r_ref, ga_ref, wo_ref, gpost_ref, gpre_ref, w1_ref, w2_ref,
                    gpost2_ref, o_ref, *, ff_tile):
    aw = a_ref.shape[1]
    d_ff = w1_ref.shape[1]
    part = x_ref.shape[0] // FFN_ROW_PARTS
    halves = [pl.ds(i * part, part) for i in range(FFN_ROW_PARTS)]
    x1 = []
    for r in halves:
        an = _rms(a_ref[r, :], ga_ref[...]).astype(BF16)
        mixed = _dot(an, wo_ref[:aw, :]) + _dot(r_ref[r, :].astype(BF16), wo_ref[aw:, :])
        x1.append(x_ref[r, :] + _rms(mixed, gpost_ref[...]))
    for r, x1h in zip(halves, x1):
        h = _rms(x1h, gpre_ref[...]).astype(BF16)
        acc = jnp.zeros(x1h.shape, F32)
        for j in range(d_ff // ff_tile):
            u = _dot(h, w1_ref[:, j * ff_tile:(j + 1) * ff_tile])
            u = jnp.square(jnp.maximum(u, 0.0)).astype(BF16)
            acc = acc + _dot(u, w2_ref[j * ff_tile:(j + 1) * ff_tile, :])
        o_ref[r, :] = x1h + _rms(acc, gpost2_ref[...])


def _out_ffn(x2, attn, rec, ga, wo_b, gpost, gpre, w1_b, w2_b, gpost2):
    n, d = x2.shape
    aw = attn.shape[1]
    rw = rec.shape[1]
    d_ff = w1_b.shape[1]

    def rows(w):
        return pl.BlockSpec((FFN_ROW_TILE, w), lambda i: (i, 0))

    def whole(shape):
        return pl.BlockSpec(shape, lambda i: (0, 0), pipeline_mode=pl.Buffered(1))

    return pl.pallas_call(
        functools.partial(_out_ffn_kernel, ff_tile=FFN_HIDDEN_TILE),
        grid=(n // FFN_ROW_TILE,),
        in_specs=[
            rows(d), rows(aw), rows(rw), whole((1, aw)), whole((aw + rw, d)), whole((1, d)),
            whole((1, d)), whole((d, d_ff)), whole((d_ff, d)), whole((1, d)),
        ],
        out_specs=rows(d),
        out_shape=jax.ShapeDtypeStruct((n, d), F32),
        compiler_params=pltpu.CompilerParams(
            dimension_semantics=("arbitrary",), vmem_limit_bytes=VMEM_LIMIT_BYTES),
        name="outproj_ffn",
    )(x2, attn, rec, ga, wo_b, gpost, gpre, w1_b, w2_b, gpost2)


def kernel(x, mix_pre_norm, w_in, attn_out_norm, hgrn_lb_logits, hgrn_out_norm, w_out,
           mix_post_norm, mlp_pre_norm, w_ff1, w_ff2, mlp_post_norm):
    batch, seq, d = x.shape
    depth = w_in.shape[0]
    aw = attn_out_norm.shape[1]
    n = batch * seq
    x2 = x.reshape(n, d)
    for layer in range(depth):
        qkv, hg, w_out_b, w_ff1_b, w_ff2_b = _inproj(
            x2, mix_pre_norm[layer][None], w_in[layer].astype(BF16), 3 * aw,
            [w_out[layer], w_ff1[layer], w_ff2[layer]])
        attn = _attention(qkv.reshape(batch, seq, -1), batch, seq)
        rec = _hgrn(hg.reshape(batch, seq, -1), hgrn_lb_logits, hgrn_out_norm[layer][None],
                    batch, seq, layer)
        x2 = _out_ffn(x2, attn.reshape(n, -1), rec.reshape(n, -1), attn_out_norm[layer][None],
                      w_out_b, mix_post_norm[layer][None], mlp_pre_norm[layer][None],
                      w_ff1_b, w_ff2_b, mlp_post_norm[layer][None])
    return x2.reshape(batch, seq, d)
```

```python
import functools

import numpy as np
import jax
import jax.numpy as jnp
from jax import lax
from jax.experimental import pallas as pl
from jax.experimental.pallas import tpu as pltpu

F32 = jnp.float32
BF16 = jnp.bfloat16

RMS_EPS = 1e-6
ATTN_HEAD_DIM = 64
ATTN_BLOCK = 128
ATTN_GROUP = 8
DILATED_PATTERNS = ((128, 1), (512, 4), (2048, 16))
HGRN_HEAD_DIM = 128
HGRN_CHUNK = 64
HGRN_SEQ_TILE = 2048
LANES = 128
SUBLANES = 8
BF16_SUBLANES = 16
MASKED_SCORE = -1e30
F32_MIN_NORMAL = 1.1754944e-38
LOG2E = 1.4426950408889634
VMEM_LIMIT_BYTES = 56 * 1024 * 1024
INPROJ_ROW_TILE = 1024
INPROJ_ROW_PARTS = 4
FFN_ROW_TILE = 1024
FFN_HIDDEN_TILE = 1024
FFN_ROW_PARTS = 4


def _rms(x, gain):
    return x * lax.rsqrt(jnp.mean(x * x, axis=-1, keepdims=True) + RMS_EPS) * gain


def _dot(a, b):
    return jnp.dot(a, b, preferred_element_type=F32)


def _dot_nt(a, b):
    return lax.dot_general(a, b, (((1,), (1,)), ((), ())), preferred_element_type=F32)


def _dot_tn(a, b):
    return lax.dot_general(a, b, (((0,), (0,)), ((), ())), preferred_element_type=F32)


def _inproj_kernel(x_ref, g_ref, w_ref, *rest, attn_cols):
    nlater = (len(rest) - 2) // 2
    later_f32, (qkv_ref, hg_ref), later_b16 = rest[:nlater], rest[nlater:nlater + 2], rest[nlater + 2:]
    for src, dst in zip(later_f32, later_b16):
        dst[...] = src[...].astype(BF16)
    part = x_ref.shape[0] // INPROJ_ROW_PARTS
    parts = [pl.ds(i * part, part) for i in range(INPROJ_ROW_PARTS)]
    normed = [_rms(x_ref[r, :], g_ref[...]).astype(BF16) for r in parts]
    for r, h in zip(parts, normed):
        qkv_ref[r, :] = _dot(h, w_ref[:, :attn_cols])
        hg_ref[r, :] = _dot(h, w_ref[:, attn_cols:])


def _inproj(x2, gain, w_in_b, attn_cols, later_weights):
    n, d = x2.shape
    width = w_in_b.shape[1]
    steps = n // INPROJ_ROW_TILE
    slices = []
    for w in later_weights:
        rows = w.shape[0] // steps
        assert rows * steps == w.shape[0] and rows % BF16_SUBLANES == 0
        slices.append(pl.BlockSpec((rows, w.shape[1]), lambda i: (i, 0)))
    return pl.pallas_call(
        functools.partial(_inproj_kernel, attn_cols=attn_cols),
        grid=(steps,),
        in_specs=[
            pl.BlockSpec((INPROJ_ROW_TILE, d), lambda i: (i, 0)),
            pl.BlockSpec((1, d), lambda i: (0, 0)),
            pl.BlockSpec((d, width), lambda i: (0, 0), pipeline_mode=pl.Buffered(1)),
        ] + slices,
        out_specs=[
            pl.BlockSpec((INPROJ_ROW_TILE, attn_cols), lambda i: (i, 0)),
            pl.BlockSpec((INPROJ_ROW_TILE, width - attn_cols), lambda i: (i, 0)),
        ] + slices,
        out_shape=[
            jax.ShapeDtypeStruct((n, attn_cols), F32),
            jax.ShapeDtypeStruct((n, width - attn_cols), F32),
        ] + [jax.ShapeDtypeStruct(w.shape, BF16) for w in later_weights],
        compiler_params=pltpu.CompilerParams(
            dimension_semantics=("arbitrary",), vmem_limit_bytes=VMEM_LIMIT_BYTES),
        name="inproj",
    )(x2, gain, w_in_b, *later_weights)


def _attn_kernel(q_ref, k_ref, v_ref, o_ref, m_sc, acc_sc, den_sc, bias_sc, *, seq):
    hp = pl.program_id(1)
    blk = ATTN_BLOCK
    patterns = sorted(DILATED_PATTERNS, key=lambda p: -p[1])
    npat = len(patterns)
    lane = lax.broadcasted_iota(jnp.int32, (blk, LANES), 1)
    head0 = lane < ATTN_HEAD_DIM
    ones_tile = jnp.ones((blk, LANES), BF16)

    row = lax.broadcasted_iota(jnp.int32, (2 * blk, 2 * blk), 0)
    col = lax.broadcasted_iota(jnp.int32, (2 * blk, 2 * blk), 1)
    qi = row & (blk - 1)
    kj = col & (blk - 1)
    is_cur = col >= blk
    dist = jnp.where(is_cur, qi - kj, qi - kj + blk)
    valid = (dist >= 0) & (dist <= blk)
    slope = jnp.exp2((-(2 * hp + (row >> 7) + 1)).astype(F32))
    for bi, (_, dil) in enumerate(patterns):
        bias = -(slope * LOG2E) * (dist * dil).astype(F32)
        bias_sc[bi] = jnp.where(valid, bias, MASKED_SCORE).astype(BF16)

    def tile_rows_of(dil, run):
        start, first, count = run
        step = blk * dil

        def rows(st):
            return pl.ds(st, blk, stride=dil) if dil > 1 else pl.ds(st, blk)

        return [None if first else rows(start - step)] + [
            rows(start + g * step) for g in range(count)]

    def group(bi, dil, runs):
        blocks = []
        for run in runs:
            tile_rows = tile_rows_of(dil, run)
            keys = [None if r is None else k_ref[0, r, :].astype(BF16) for r in tile_rows]
            vals = [None if r is None else
                    jnp.concatenate([v_ref[0, r, :].astype(BF16), ones_tile], axis=1)
                    for r in tile_rows]
            for g in range(run[2]):
                blocks.append((tile_rows[g + 1], keys[g], keys[g + 1], vals[g], vals[g + 1]))
        scores = []
        for r, kp, kc, _, _ in blocks:
            qb = q_ref[0, r, :] * (ATTN_HEAD_DIM ** -0.5 * LOG2E)
            qs = jnp.concatenate([jnp.where(head0, qb, 0.0), jnp.where(head0, 0.0, qb)],
                                 axis=0).astype(BF16)
            if kp is None:
                scores.append(_dot_nt(qs, kc).astype(BF16) + bias_sc[bi, :, blk:])
            else:
                kcat = jnp.concatenate([kp, kc], axis=0)
                scores.append(_dot_nt(qs, kcat).astype(BF16) + bias_sc[bi])
        maxes = [jnp.max(s, axis=-1, keepdims=True) for s in scores]
        probs = [jnp.exp2(s - mx) for s, mx in zip(scores, maxes)]
        outs = []
        for (_, _, _, vp, vc), p in zip(blocks, probs):
            outs.append(_dot(p, vc) if vp is None else _dot(p, jnp.concatenate([vp, vc], axis=0)))
        for (r, _, _, _, _), out, mx in zip(blocks, outs, maxes):
            mx = mx.astype(F32)
            acc = jnp.where(head0, out[:blk, :LANES], out[blk:, :LANES])
            den = jnp.where(head0, out[:blk, LANES:], out[blk:, LANES:])
            mb = jnp.where(head0, jnp.broadcast_to(mx[:blk], (blk, LANES)),
                           jnp.broadcast_to(mx[blk:], (blk, LANES)))
            if bi > 0:
                m_old = m_sc[r, :]
                m_new = jnp.maximum(m_old, mb)
                w_old = jnp.exp2(m_old - m_new)
                w_new = jnp.exp2(mb - m_new)
                acc = acc_sc[r, :] * w_old + acc * w_new
                den = den_sc[r, :] * w_old + den * w_new
                mb = m_new
            if bi < npat - 1:
                m_sc[r, :] = mb
                acc_sc[r, :] = acc
                den_sc[r, :] = den
            else:
                o_ref[0, r, :] = acc / den

    for bi, (_, dil) in enumerate(patterns):
        run_len = seq // dil // blk
        for g in range(seq // blk // ATTN_GROUP):
            if run_len >= ATTN_GROUP:
                per_residue = run_len // ATTN_GROUP
                i = g % per_residue
                runs = [(g // per_residue + i * (ATTN_GROUP * blk * dil), i == 0, ATTN_GROUP)]
            else:
                nruns = ATTN_GROUP // run_len
                runs = [(g * nruns + u, True, run_len) for u in range(nruns)]
            group(bi, dil, runs)


def _attention(qkv, batch, seq):
    width = qkv.shape[-1] // 3
    pairs = width // LANES
    blk = ATTN_BLOCK
    return pl.pallas_call(
        functools.partial(_attn_kernel, seq=seq),
        grid=(batch, pairs),
        in_specs=[
            pl.BlockSpec((1, seq, LANES), lambda b, h: (b, 0, h)),
            pl.BlockSpec((1, seq, LANES), lambda b, h: (b, 0, pairs + h)),
            pl.BlockSpec((1, seq, LANES), lambda b, h: (b, 0, 2 * pairs + h)),
        ],
        out_specs=pl.BlockSpec((1, seq, LANES), lambda b, h: (b, 0, h)),
        out_shape=jax.ShapeDtypeStruct((batch, seq, width), F32),
        scratch_shapes=[
            pltpu.VMEM((seq, LANES), F32),
            pltpu.VMEM((seq, LANES), F32),
            pltpu.VMEM((seq, LANES), F32),
            pltpu.VMEM((len(DILATED_PATTERNS), 2 * blk, 2 * blk), BF16),
        ],
        compiler_params=pltpu.CompilerParams(
            dimension_semantics=("arbitrary", "arbitrary"), vmem_limit_bytes=VMEM_LIMIT_BYTES),
        name="dilated_attention",
    )(qkv, qkv, qkv)


def _hgrn_tables(chunk):
    levels = int(np.log2(chunk))
    t = np.arange(chunk)
    masks = np.zeros((levels + 1, chunk, chunk), np.float32)
    for l in range(levels):
        blk = t // (1 << l)
        masks[l] = ((blk[:, None] & 1) == 1) & (blk[None, :] == blk[:, None] - 1)
    masks[levels] = np.eye(chunk, dtype=np.float32)
    assert np.array_equal(masks.sum(0), np.tril(np.ones((chunk, chunk), np.float32)))
    return np.tril(np.ones((chunk, chunk), np.float32)), masks


def _hgrn_kernel(q_ref, f_ref, i_ref, lbl_ref, tri_ref, masks_ref,
                 o_ref, state_sc, *b_slots, tile, chunk, layer, heads):
    hd = HGRN_HEAD_DIM
    levels = masks_ref.shape[0] - 1

    @pl.when(pl.program_id(1) == 0)
    def _():
        state_sc[...] = jnp.zeros(state_sc.shape, F32)

    logits = lbl_ref[...]
    e = jnp.exp(logits - jnp.max(logits, axis=0, keepdims=True))
    lb_all = jnp.sum(e[0:layer + 1, :], axis=0, keepdims=True) / jnp.sum(e, axis=0, keepdims=True)
    row = lax.broadcasted_iota(jnp.int32, (chunk, hd), 0)
    odd_row = (row & 1) == 1
    low_half = (row & 4) == 0
    tri = tri_ref[...]

    role_sign = {l: jnp.where(((row >> l) & 1) == 1, 1.0, -1.0)
                 for l in range(1, levels) if (1 << l) < SUBLANES}

    def level_exponent(b_sc, h, l):
        m = 1 << l

        def bcast(r):
            return jnp.broadcast_to(b_sc[h, r:r + 1, :], (SUBLANES, hd))
        pieces = []
        for s in range(chunk // SUBLANES):
            first = s * SUBLANES
            slab = b_sc[h, first:first + SUBLANES, :]
            if m >= SUBLANES:
                bd = bcast(first // (2 * m) * (2 * m) + m - 1)
                pieces.append(slab - bd if (first // m) % 2 == 1 else bd - slab)
            else:
                if m == 4:
                    bd = bcast(first + 3)
                else:
                    bd = jnp.where(low_half[:SUBLANES], bcast(first + 1), bcast(first + 5))
                pieces.append((slab - bd) * role_sign[l][:SUBLANES])
        return jnp.concatenate(pieces, axis=0)

    def prepare(c):
        b_sc = b_slots[c % 2]
        rows = pl.ds(c * chunk, chunk)
        qk, d0, v = [], [], []
        for h in range(heads):
            sl = slice(h * hd, (h + 1) * hd)
            lb = lb_all[:, sl]
            qr = q_ref[0, rows, sl]
            open_part = (1.0 - lb) * jax.nn.sigmoid(f_ref[0, rows, sl])
            f = lb + open_part
            qk.append(jnp.concatenate([qr * jax.nn.sigmoid(qr), (1.0 - lb) - open_part],
                                      axis=0).astype(BF16))
            d0.append(jnp.where(odd_row, f, 1.0).astype(BF16))
            v.append(i_ref[0, rows, sl].astype(BF16))
            g = jnp.log2(jnp.maximum(f, F32_MIN_NORMAL))
            g_hi = g.astype(BF16)
            g_lo = (g - g_hi.astype(F32)).astype(BF16)
            b_sc[h] = _dot(tri, g_hi) + _dot(tri, g_lo)
        return qk, d0, v

    def mix(c, qk, d0, v):
        b_sc = b_slots[c % 2]
        o_inter = []
        for h in range(heads):
            b = b_sc[h]
            state_t = state_sc[h]
            q_dec = qk[h][:chunk] * jnp.exp2(b).astype(BF16)
            o_inter.append(_dot_nt(q_dec, state_t.astype(BF16)))
            b_last = b[chunk - 1:chunk, :]
            k_end = qk[h][chunk:] * jnp.exp2(b_last - b).astype(BF16)
            state_sc[h] = state_t * jnp.exp2(b_last) + _dot_tn(v[h], k_end)
        a = []
        for h in range(heads):
            ah = masks_ref[levels] * _dot_nt(qk[h][:chunk], qk[h][chunk:])
            for l in range(levels):
                dec = d0[h] if l == 0 else jnp.exp2(level_exponent(b_sc, h, l)).astype(BF16)
                prod = qk[h] * jnp.concatenate([dec, dec], axis=0)
                ah = ah + masks_ref[l] * _dot_nt(prod[:chunk], prod[chunk:])
            a.append(ah)
        return a, v, o_inter

    def finish(c, a, v, o_inter):
        rows = pl.ds(c * chunk, chunk)
        for h in range(heads):
            o_ref[0, rows, h * hd:(h + 1) * hd] = _dot(a[h].astype(BF16), v[h]) + o_inter[h]

    nchunks = tile // chunk
    prepared = {0: prepare(0)}
    mixed = {}
    for c in range(nchunks + 1):
        if c + 1 < nchunks:
            prepared[c + 1] = prepare(c + 1)
        if c < nchunks:
            mixed[c] = mix(c, *prepared.pop(c))
        if c - 1 in mixed:
            finish(c - 1, *mixed.pop(c - 1))


def _hgrn(hg, lb_logits, batch, seq, layer):
    width = hg.shape[-1] // 4
    heads = width // HGRN_HEAD_DIM
    chunk = HGRN_CHUNK
    tile = HGRN_SEQ_TILE
    tri, masks = _hgrn_tables(chunk)
    nrows = lb_logits.shape[0]

    def col(k):
        return pl.BlockSpec((1, tile, width), lambda b, t, k=k: (b, t, k))

    return pl.pallas_call(
        functools.partial(_hgrn_kernel, tile=tile, chunk=chunk, layer=layer, heads=heads),
        grid=(batch, seq // tile),
        in_specs=[
            col(0), col(1), col(2),
            pl.BlockSpec((nrows, width), lambda b, t: (0, 0)),
            pl.BlockSpec(tri.shape, lambda b, t: (0, 0)),
            pl.BlockSpec(masks.shape, lambda b, t: (0, 0, 0)),
        ],
        out_specs=pl.BlockSpec((1, tile, width), lambda b, t: (b, t, 0)),
        out_shape=jax.ShapeDtypeStruct((batch, seq, width), F32),
        scratch_shapes=(
            [pltpu.VMEM((heads, HGRN_HEAD_DIM, HGRN_HEAD_DIM), F32)]
            + 2 * [pltpu.VMEM((heads, chunk, HGRN_HEAD_DIM), F32)]),
        compiler_params=pltpu.CompilerParams(
            dimension_semantics=("arbitrary", "arbitrary"), vmem_limit_bytes=VMEM_LIMIT_BYTES),
        name="hgrn2",
    )(hg, hg, hg, lb_logits, jnp.asarray(tri, BF16), jnp.asarray(masks, F32))


def _out_ffn_kernel(x_ref, a_ref, r_ref, gate_ref, ga_ref, gr_ref, wo_ref, gpost_ref, gpre_ref,
                    w1_ref, w2_ref, gpost2_ref, o_ref, *, ff_tile):
    aw = a_ref.shape[1]
    d_ff = w1_ref.shape[1]
    part = x_ref.shape[0] // FFN_ROW_PARTS
    hd = HGRN_HEAD_DIM
    halves = [pl.ds(i * part, part) for i in range(FFN_ROW_PARTS)]
    x1 = []
    for r in halves:
        an = _rms(a_ref[r, :], ga_ref[...]).astype(BF16)
        gt = gate_ref[r, :]
        rec = jnp.concatenate(
            [_rms(r_ref[r, h * hd:(h + 1) * hd], gr_ref[:, h * hd:(h + 1) * hd])
             for h in range(r_ref.shape[1] // hd)], axis=1) * (gt * jax.nn.sigmoid(gt))
        mixed = _dot(an, wo_ref[:aw, :]) + _dot(rec.astype(BF16), wo_ref[aw:, :])
        x1.append(x_ref[r, :] + _rms(mixed, gpost_ref[...]))
    for r, x1h in zip(halves, x1):
        h = _rms(x1h, gpre_ref[...]).astype(BF16)
        acc = jnp.zeros(x1h.shape, F32)
        for j in range(d_ff // ff_tile):
            u = _dot(h, w1_ref[:, j * ff_tile:(j + 1) * ff_tile])
            u = jnp.square(jnp.maximum(u, 0.0)).astype(BF16)
            acc = acc + _dot(u, w2_ref[j * ff_tile:(j + 1) * ff_tile, :])
        o_ref[r, :] = x1h + _rms(acc, gpost2_ref[...])


def _out_ffn(x2, attn, rec, hg2, ga, gr, wo_b, gpost, gpre, w1_b, w2_b, gpost2):
    n, d = x2.shape
    aw = attn.shape[1]
    rw = rec.shape[1]
    d_ff = w1_b.shape[1]
    gate_block = hg2.shape[1] // rw - 1

    def rows(w):
        return pl.BlockSpec((FFN_ROW_TILE, w), lambda i: (i, 0))

    def whole(shape):
        return pl.BlockSpec(shape, lambda i: (0, 0), pipeline_mode=pl.Buffered(1))

    return pl.pallas_call(
        functools.partial(_out_ffn_kernel, ff_tile=FFN_HIDDEN_TILE),
        grid=(n // FFN_ROW_TILE,),
        in_specs=[
            rows(d), rows(aw), rows(rw),
            pl.BlockSpec((FFN_ROW_TILE, rw), lambda i: (i, gate_block)),
            whole((1, aw)), whole((1, rw)), whole((aw + rw, d)), whole((1, d)),
            whole((1, d)), whole((d, d_ff)), whole((d_ff, d)), whole((1, d)),
        ],
        out_specs=rows(d),
        out_shape=jax.ShapeDtypeStruct((n, d), F32),
        compiler_params=pltpu.CompilerParams(
            dimension_semantics=("arbitrary",), vmem_limit_bytes=VMEM_LIMIT_BYTES),
        name="outproj_ffn",
    )(x2, attn, rec, hg2, ga, gr, wo_b, gpost, gpre, w1_b, w2_b, gpost2)


def kernel(x, mix_pre_norm, w_in, attn_out_norm, hgrn_lb_logits, hgrn_out_norm, w_out,
           mix_post_norm, mlp_pre_norm, w_ff1, w_ff2, mlp_post_norm):
    batch, seq, d = x.shape
    depth = w_in.shape[0]
    aw = attn_out_norm.shape[1]
    n = batch * seq
    x2 = x.reshape(n, d)
    for layer in range(depth):
        qkv, hg, w_out_b, w_ff1_b, w_ff2_b = _inproj(
            x2, mix_pre_norm[layer][None], w_in[layer].astype(BF16), 3 * aw,
            [w_out[layer], w_ff1[layer], w_ff2[layer]])
        attn = _attention(qkv.reshape(batch, seq, -1), batch, seq)
        rec = _hgrn(hg.reshape(batch, seq, -1), hgrn_lb_logits, batch, seq, layer)
        x2 = _out_ffn(x2, attn.reshape(n, -1), rec.reshape(n, -1), hg,
                      attn_out_norm[layer][None], hgrn_out_norm[layer][None],
                      w_out_b, mix_post_norm[layer][None], mlp_pre_norm[layer][None],
                      w_ff1_b, w_ff2_b, mlp_post_norm[layer][None])
    return x2.reshape(batch, seq, d)
```
